```python
import jax
import jax.numpy as jnp
from jax import lax
import numpy as np

D_MODEL = 2048
BATCH = 1
SEQ = 16384
DEPTH = 2

GRID_W = 64
CTX_LEN = 256
EPS = 1e-6

GLA_HEADS = 4
GLA_DK = 64
GLA_DV = 128
GLA_GATE_RANK = 16
GLA_TAU = 16.0
GLA_CHUNK = 64
ROPE_BASE = 10000.0

NA_HEADS = 8
NA_HD = 128
NA_KH = 8
NA_KW = 16

LRU_WIDTH = 512
LRU_BLOCKS = 4
LRU_CONV = 4
LRU_C = 8.0

N_EXPERTS = 16
EXPERT_FF = 2048
CAPACITY = 2

GLA_QK = GLA_HEADS * GLA_DK
GLA_V = GLA_HEADS * GLA_DV
NA_W = NA_HEADS * NA_HD
MIX_W = GLA_V + NA_W + LRU_WIDTH
IN_SIZES = (GLA_QK, GLA_QK, GLA_V, GLA_V, GLA_GATE_RANK, GLA_GATE_RANK, NA_W, NA_W, NA_W, LRU_WIDTH, LRU_WIDTH)
D_IN = sum(IN_SIZES)

kernel_name = "hybrid_gla_natten_rglru_ecmoe_dit"


def rms_norm(x, gain):
    xf = x.astype(jnp.float32)
    y = xf * lax.rsqrt(jnp.mean(xf * xf, axis=-1, keepdims=True) + EPS)
    return (y * gain.astype(jnp.float32)).astype(x.dtype)


def modulate(h, shift, scale):
    return h * (1 + scale) + shift


def split_in(z):
    out, o = [], 0
    for s in IN_SIZES:
        out.append(z[..., o:o + s])
        o += s
    return out


def to_heads(t, n):
    b, T, _ = t.shape
    return t.reshape(b, T, n, -1).transpose(0, 2, 1, 3)


def from_heads(t):
    b, h, T, d = t.shape
    return t.transpose(0, 2, 1, 3).reshape(b, T, h * d)


def axial_rope(n_tok, dim):
    pos = jnp.arange(n_tok)
    row = (pos // GRID_W).astype(jnp.float32)
    col = (pos % GRID_W).astype(jnp.float32)
    nf = dim // 4
    inv = ROPE_BASE ** (-jnp.arange(nf, dtype=jnp.float32) / nf)
    ang = jnp.concatenate([row[:, None] * inv, col[:, None] * inv], axis=-1)
    return jnp.cos(ang), jnp.sin(ang)


def apply_rope(t, cos, sin):
    t1, t2 = jnp.split(t, 2, axis=-1)
    return jnp.concatenate([t1 * cos - t2 * sin, t1 * sin + t2 * cos], axis=-1)


def gla_decay(lr, w_g, b_g):
    z = (lr @ w_g + b_g).astype(jnp.float32)
    return to_heads(jax.nn.log_sigmoid(z) / GLA_TAU, GLA_HEADS)


def gla_chunked(q, k, v, g, s0):
    B, H, T, dk = q.shape
    dv = v.shape[-1]
    C = GLA_CHUNK
    n = T // C
    q, k, g = (a.reshape(B, H, n, C, dk) for a in (q, k, g))
    v = v.reshape(B, H, n, C, dv)
    gc = jnp.cumsum(g, axis=3)
    g_last = gc[:, :, :, -1:, :]
    qe = q * jnp.exp(gc)
    ke = k * jnp.exp(-gc)
    kd = k * jnp.exp(g_last - gc)
    mask = jnp.tril(jnp.ones((C, C), dtype=bool))
    att = jnp.where(mask, jnp.einsum('bhnid,bhnjd->bhnij', qe, ke), 0.0)
    o_intra = jnp.einsum('bhnij,bhnjv->bhniv', att, v)
    ds = jnp.einsum('bhncd,bhncv->bhndv', kd, v)
    decay = jnp.exp(g_last[:, :, :, 0, :])

    def step(s, inp):
        dec, d = inp
        return dec[..., None] * s + d, s

    s_fin, s_prev = lax.scan(step, s0, (jnp.moveaxis(decay, 2, 0), jnp.moveaxis(ds, 2, 0)))
    s_prev = jnp.moveaxis(s_prev, 0, 2)
    o = o_intra + jnp.einsum('bhncd,bhndv->bhncv', qe, s_prev)
    return o.reshape(B, H, T, dv), s_fin


def gla_state(k, v, g):
    G = jnp.cumsum(g, axis=2)
    w = jnp.exp(G[:, :, -1:, :] - G)
    return jnp.einsum('bhtd,bhtv->bhdv', k * w, v)


def gla_mixer(q, k, v, r, lf, lb, cq, ck, cv, cr, clf, clb, w_gate, b_gate, norm_g, cos, sin, ctx_out):
    f32 = jnp.float32
    scale = GLA_DK ** -0.5
    ql = apply_rope(to_heads(q, GLA_HEADS).astype(f32), cos, sin) * scale
    kl = apply_rope(to_heads(k, GLA_HEADS).astype(f32), cos, sin)
    vl = to_heads(v, GLA_HEADS).astype(f32)
    kc = to_heads(ck, GLA_HEADS).astype(f32)
    vc = to_heads(cv, GLA_HEADS).astype(f32)
    dec_l = (gla_decay(lf, w_gate[0], b_gate[0]), gla_decay(lb, w_gate[1], b_gate[1]))
    dec_c = (gla_decay(clf, w_gate[0], b_gate[0]), gla_decay(clb, w_gate[1], b_gate[1]))
    qc = to_heads(cq, GLA_HEADS).astype(f32) * scale if ctx_out else None
    zeros = jnp.zeros(kc.shape[:2] + (GLA_DK, GLA_DV), f32)
    o_l, o_c = 0.0, 0.0
    for d in range(2):
        fl = (lambda t: jnp.flip(t, axis=2)) if d else (lambda t: t)
        if ctx_out:
            oc_d, s_c = gla_chunked(fl(qc), fl(kc), fl(vc), fl(dec_c[d]), zeros)
            o_c = o_c + fl(oc_d)
        else:
            s_c = gla_state(fl(kc), fl(vc), fl(dec_c[d]))
        ol_d, _ = gla_chunked(fl(ql), fl(kl), fl(vl), fl(dec_l[d]), s_c)
        o_l = o_l + fl(ol_d)
    out_l = from_heads(rms_norm(o_l, norm_g)) * jax.nn.silu(r.astype(f32))
    out_c = from_heads(rms_norm(o_c, norm_g)) * jax.nn.silu(cr.astype(f32)) if ctx_out else None
    return out_l, out_c


def na_latent(q, k, v, kc, vc, rpb):
    B, T, H, hd = q.shape
    rows = T // GRID_W
    kh = min(NA_KH, rows)
    kw = NA_KW
    qg = q.reshape(B, rows, GRID_W, H, hd)
    kg = k.reshape(B, rows, GRID_W, H, hd)
    vg = v.reshape(B, rows, GRID_W, H, hd)
    cols = jnp.arange(GRID_W)
    col_start = jnp.clip(cols - kw // 2, 0, GRID_W - kw)
    col_idx = col_start[:, None] + jnp.arange(kw)
    col_bias_idx = col_idx - cols[:, None] + (NA_KW - 1)
    scale = hd ** -0.5

    def row_block(r):
        rs = jnp.clip(r - kh // 2, 0, rows - kh)
        kb = lax.dynamic_slice_in_dim(kg, rs, kh, axis=1)[:, :, col_idx]
        vb = lax.dynamic_slice_in_dim(vg, rs, kh, axis=1)[:, :, col_idx]
        qb = lax.dynamic_index_in_dim(qg, r, axis=1, keepdims=False)
        row_bias_idx = rs + jnp.arange(kh) - r + (NA_KH - 1)
        bias = rpb[:, row_bias_idx][:, :, col_bias_idx]
        bias = bias.transpose(0, 2, 1, 3).reshape(H, GRID_W, kh * kw)
        s_loc = jnp.einsum('bqhd,bkqwhd->bhqkw', qb, kb).reshape(B, H, GRID_W, kh * kw) * scale + bias
        s_ctx = jnp.einsum('bqhd,bchd->bhqc', qb, kc) * scale
        p = jax.nn.softmax(jnp.concatenate([s_loc, s_ctx], axis=-1).astype(jnp.float32), axis=-1).astype(v.dtype)
        p_loc = p[..., :kh * kw].reshape(B, H, GRID_W, kh, kw)
        p_ctx = p[..., kh * kw:]
        return (jnp.einsum('bhqkw,bkqwhd->bqhd', p_loc, vb)
                + jnp.einsum('bhqc,bchd->bqhd', p_ctx, vc))

    o = lax.map(row_block, jnp.arange(rows))
    return o.transpose(1, 0, 2, 3, 4).reshape(B, T, H * hd)


def na_mixer(q, k, v, cq, ck, cv, qn, kn, rpb, ctx_out):
    heads = lambda t: t.reshape(t.shape[0], t.shape[1], NA_HEADS, NA_HD)
    ql, kl, vl = rms_norm(heads(q), qn), rms_norm(heads(k), kn), heads(v)
    kc, vc = rms_norm(heads(ck), kn), heads(cv)
    out_l = na_latent(ql, kl, vl, kc, vc, rpb)
    out_c = None
    if ctx_out:
        qc = rms_norm(heads(cq), qn)
        s = jnp.einsum('bqhd,bkhd->bhqk', qc, kc) * NA_HD ** -0.5
        p = jax.nn.softmax(s.astype(jnp.float32), axis=-1).astype(vc.dtype)
        out_c = jnp.einsum('bhqk,bkhd->bqhd', p, vc).reshape(cq.shape[0], cq.shape[1], NA_W)
    return out_l, out_c


def centred_conv(x, w, b):
    C = x.shape[-1]
    y = lax.conv_general_dilated(x, w[:, None, :].astype(x.dtype), window_strides=(1,),
                                 padding=[((LRU_CONV - 1) // 2, LRU_CONV // 2)],
                                 dimension_numbers=('NWC', 'WIO', 'NWC'), feature_group_count=C)
    return y + b


def rglru_coeffs(xc, w_a, b_a, w_i, b_i, lam):
    B, T, C = xc.shape
    xb = xc.reshape(B, T, LRU_BLOCKS, C // LRU_BLOCKS)
    r = jax.nn.sigmoid(jnp.einsum('btnc,ncd->btnd', xb, w_a).reshape(B, T, C) + b_a)
    i = jax.nn.sigmoid(jnp.einsum('btnc,ncd->btnd', xb, w_i).reshape(B, T, C) + b_i)
    log_a = LRU_C * r * jax.nn.log_sigmoid(lam)
    a = jnp.exp(log_a)
    bterm = jnp.sqrt(-jnp.expm1(2.0 * log_a)) * (i * xc)
    return a, bterm


def linear_scan(a, b, h0):
    b = b.at[:, 0].add(a[:, 0] * h0)

    def comb(lhs, rhs):
        al, bl = lhs
        ar, br = rhs
        return al * ar, ar * bl + br

    _, h = lax.associative_scan(comb, (a, b), axis=1)
    return h


def lru_mixer(xl, yl, xc, yc, conv_w, conv_b, w_a, b_a, w_i, b_i, lam, ctx_out):
    f32 = jnp.float32
    xl = centred_conv(xl, conv_w, conv_b).astype(f32)
    xc = centred_conv(xc, conv_w, conv_b).astype(f32)
    hl, hc = 0.0, 0.0
    for d in range(2):
        fl = (lambda t: jnp.flip(t, axis=1)) if d else (lambda t: t)
        ac, bc = rglru_coeffs(fl(xc), w_a[d], b_a[d], w_i[d], b_i[d], lam[d])
        h_c = linear_scan(ac, bc, jnp.zeros_like(bc[:, 0]))
        al, bl = rglru_coeffs(fl(xl), w_a[d], b_a[d], w_i[d], b_i[d], lam[d])
        h_l = linear_scan(al, bl, h_c[:, -1])
        hl = hl + fl(h_l)
        if ctx_out:
            hc = hc + fl(h_c)
    out_l = hl * jax.nn.gelu(yl.astype(f32))
    out_c = hc * jax.nn.gelu(yc.astype(f32)) if ctx_out else None
    return out_l, out_c


def expert_choice(h, w_router, w_gate, w_up, w_down):
    B, N, D = h.shape
    cap = CAPACITY * N // N_EXPERTS
    aff = jax.nn.softmax((h @ w_router).astype(jnp.float32), axis=-1)
    top_w, top_idx = lax.top_k(jnp.swapaxes(aff, 1, 2), cap)
    xs = jax.vmap(lambda hb, ib: hb[ib])(h, top_idx)
    hid = jax.nn.silu(jnp.einsum('becd,edf->becf', xs, w_gate)) * jnp.einsum('becd,edf->becf', xs, w_up)
    ye = jnp.einsum('becf,efd->becd', hid, w_down) * top_w[..., None].astype(h.dtype)
    return jax.vmap(lambda yb, ib: jnp.zeros((N, D), yb.dtype).at[ib.reshape(-1)].add(yb.reshape(-1, D)))(ye, top_idx)


def setup_inputs(seed: int = 0) -> dict:
    key = jax.random.key(seed)
    counter = [0]

    def nrm(shape, scale):
        counter[0] += 1
        return jax.random.normal(jax.random.fold_in(key, counter[0]), shape, jnp.float32) * scale

    L, D = DEPTH, D_MODEL
    bw = LRU_WIDTH // LRU_BLOCKS
    counter[0] += 1
    u = jax.random.uniform(jax.random.fold_in(key, counter[0]), (L, 2, LRU_WIDTH), jnp.float32, 0.9, 0.999)
    s = u ** (1.0 / LRU_C)
    return {
        "x": nrm((BATCH, SEQ, D), 1.0),
        "c": nrm((BATCH, D), 1.0),
        "ctx": nrm((BATCH, CTX_LEN, D), 1.0),
        "c_ctx": nrm((D,), 1.0),
        "norm1_g": 1.0 + nrm((L, D), 0.02),
        "norm2_g": 1.0 + nrm((L, D), 0.02),
        "w_mod": nrm((L, D, 6 * D), 0.5 * D ** -0.5),
        "b_mod": nrm((L, 6 * D), 0.02),
        "w_in": nrm((L, D, D_IN), D ** -0.5),
        "gla_w_gate": nrm((L, 2, GLA_GATE_RANK, GLA_QK), GLA_GATE_RANK ** -0.5),
        "gla_b_gate": nrm((L, 2, GLA_QK), 0.5),
        "gla_norm_g": 1.0 + nrm((L, GLA_DV), 0.02),
        "na_q_norm_g": 1.0 + nrm((L, NA_HD), 0.02),
        "na_k_norm_g": 1.0 + nrm((L, NA_HD), 0.02),
        "na_rpb": nrm((L, NA_HEADS, 2 * NA_KH - 1, 2 * NA_KW - 1), 0.1),
        "lru_conv_w": nrm((L, LRU_CONV, LRU_WIDTH), LRU_CONV ** -0.5),
        "lru_conv_b": nrm((L, LRU_WIDTH), 0.02),
        "lru_w_a": nrm((L, 2, LRU_BLOCKS, bw, bw), bw ** -0.5),
        "lru_b_a": nrm((L, 2, LRU_WIDTH), 0.1),
        "lru_w_i": nrm((L, 2, LRU_BLOCKS, bw, bw), bw ** -0.5),
        "lru_b_i": nrm((L, 2, LRU_WIDTH), 0.1),
        "lru_lambda": jnp.log(s) - jnp.log1p(-s),
        "w_out": nrm((L, MIX_W, D), MIX_W ** -0.5),
        "w_router": nrm((L, D, N_EXPERTS), D ** -0.5),
        "w_exp_gate": nrm((L, N_EXPERTS, D, EXPERT_FF), D ** -0.5),
        "w_exp_up": nrm((L, N_EXPERTS, D, EXPERT_FF), D ** -0.5),
        "w_exp_down": nrm((L, N_EXPERTS, EXPERT_FF, D), EXPERT_FF ** -0.5),
    }


def reference(x, c, ctx, c_ctx, norm1_g, norm2_g, w_mod, b_mod, w_in, gla_w_gate, gla_b_gate, gla_norm_g,
              na_q_norm_g, na_k_norm_g, na_rpb, lru_conv_w, lru_conv_b, lru_w_a, lru_b_a, lru_w_i, lru_b_i,
              lru_lambda, w_out, w_router, w_exp_gate, w_exp_up, w_exp_down):
    cos, sin = axial_rope(x.shape[1], GLA_DK)
    for l in range(DEPTH):
        ctx_out = l < DEPTH - 1
        mx = jnp.split((jax.nn.silu(c) @ w_mod[l] + b_mod[l])[:, None, :], 6, axis=-1)
        mc = jnp.split((jax.nn.silu(c_ctx) @ w_mod[l] + b_mod[l])[None, None, :], 6, axis=-1)
        zx = split_in(modulate(rms_norm(x, norm1_g[l]), mx[0], mx[1]) @ w_in[l])
        zc = split_in(modulate(rms_norm(ctx, norm1_g[l]), mc[0], mc[1]) @ w_in[l])
        gla_x, gla_c = gla_mixer(*zx[0:6], *zc[0:6], gla_w_gate[l], gla_b_gate[l], gla_norm_g[l], cos, sin, ctx_out)
        na_x, na_c = na_mixer(*zx[6:9], *zc[6:9], na_q_norm_g[l], na_k_norm_g[l], na_rpb[l], ctx_out)
        lru_x, lru_c = lru_mixer(*zx[9:11], *zc[9:11], lru_conv_w[l], lru_conv_b[l], lru_w_a[l], lru_b_a[l],
                                 lru_w_i[l], lru_b_i[l], lru_lambda[l], ctx_out)
        mix_x = jnp.concatenate([gla_x, na_x.astype(jnp.float32), lru_x], axis=-1).astype(x.dtype)
        x = x + mx[2] * (mix_x @ w_out[l])
        x = x + mx[5] * expert_choice(modulate(rms_norm(x, norm2_g[l]), mx[3], mx[4]),
                                      w_router[l], w_exp_gate[l], w_exp_up[l], w_exp_down[l])
        if ctx_out:
            mix_c = jnp.concatenate([gla_c, na_c.astype(jnp.float32), lru_c], axis=-1).astype(ctx.dtype)
            ctx = ctx + mc[2] * (mix_c @ w_out[l])
            ctx = ctx + mc[5] * expert_choice(modulate(rms_norm(ctx, norm2_g[l]), mc[3], mc[4]),
                                              w_router[l], w_exp_gate[l], w_exp_up[l], w_exp_down[l])
    return x
```

```python
import functools

import numpy as np
import jax
import jax.numpy as jnp
from jax import lax
from jax.experimental import pallas as pl
from jax.experimental.pallas import tpu as pltpu

F32 = jnp.float32
BF16 = jnp.bfloat16

GRID_W = 64
EPS = 1e-6
GLA_HEADS = 4
GLA_DK = 64
GLA_DV = 128
GLA_GATE_RANK = 16
GLA_TAU = 16.0
GLA_CHUNK = 64
ROPE_BASE = 10000.0
NA_HEADS = 8
NA_HD = 128
NA_KH = 8
NA_KW = 16
LRU_WIDTH = 512
LRU_BLOCKS = 4
LRU_CONV = 4
LRU_C = 8.0
N_EXPERTS = 16
CAPACITY = 2

GLA_QK = GLA_HEADS * GLA_DK
GLA_V = GLA_HEADS * GLA_DV
NA_W = NA_HEADS * NA_HD

LANES = 128
SUBLANES = 8
VMEM_LIMIT = 56 * 1024 * 1024
NEG = -1e30

Z_NAQ, Z_NAK, Z_NAV = 0, NA_W, 2 * NA_W
Z_GV = 3 * NA_W
Z_GR = Z_GV + GLA_V
Z_LX = Z_GR + GLA_V
Z_LY = Z_LX + LRU_WIDTH
Z_GQ = Z_LY + LRU_WIDTH
Z_GK = Z_GQ + GLA_QK
Z_LR = Z_GK + GLA_QK
Z_COLS = Z_LR + LANES


def _params(*sem):
    return pltpu.CompilerParams(dimension_semantics=sem, vmem_limit_bytes=VMEM_LIMIT)


def _dot(a, b):
    return jnp.dot(a, b, preferred_element_type=F32)


def _dot_nt(a, b):
    return lax.dot_general(a, b, (((1,), (1,)), ((), ())), preferred_element_type=F32)


def _dot_tn(a, b):
    return lax.dot_general(a, b, (((0,), (0,)), ((), ())), preferred_element_type=F32)


def _split2(a):
    hi = a.astype(BF16)
    lo = (a - hi.astype(F32)).astype(BF16)
    return hi, lo


def _dot_hi(a, b):
    ah, al = _split2(a)
    bh, bl = _split2(b)
    return _dot(ah, bh) + _dot(al, bh) + _dot(ah, bl)


def _rms(x):
    return x * lax.rsqrt(jnp.mean(x * x, axis=-1, keepdims=True) + EPS)


def _log_sigmoid(z):
    return jnp.minimum(z, 0.0) - jnp.log1p(jnp.exp(-jnp.abs(z)))


def _mod_kernel(cb_ref, w_ref, b_ref, o_ref):
    tn = w_ref.shape[2]
    s = [cb_ref[v] * jax.nn.sigmoid(cb_ref[v]) for v in range(2)]
    for j in range(tn // LANES):
        cs = slice(j * LANES, (j + 1) * LANES)
        wj = w_ref[0, :, cs]
        for v in range(2):
            o_ref[0, v:v + 1, cs] = jnp.sum(wj * s[v], axis=0, keepdims=True) + b_ref[0, :, cs]


def _modulation(c2, w_mod, b_mod):
    L, D, N6 = w_mod.shape
    tn = 1024
    cb = jnp.broadcast_to(c2[:, :, None], (2, D, LANES))
    return pl.pallas_call(
        _mod_kernel,
        grid=(L, N6 // tn),
        in_specs=[pl.BlockSpec((2, D, LANES), lambda l, j: (0, 0, 0)),
                  pl.BlockSpec((1, D, tn), lambda l, j: (l, 0, j)),
                  pl.BlockSpec((1, 1, tn), lambda l, j: (l, 0, j))],
        out_specs=pl.BlockSpec((1, 2, tn), lambda l, j: (l, 0, j)),
        out_shape=jax.ShapeDtypeStruct((L, 2, N6), F32),
        compiler_params=_params("arbitrary", "arbitrary"),
    )(cb, w_mod, b_mod.reshape(L, 1, N6))


def _inproj_kernel(x_ref, g_ref, sh_ref, sc_ref, w_ref, o_ref, h_scr):
    @pl.when(pl.program_id(1) == 0)
    def _():
        y = _rms(x_ref[...])
        h_scr[...] = ((y * g_ref[...]) * (1.0 + sc_ref[...]) + sh_ref[...]).astype(BF16)

    o_ref[...] = _dot(h_scr[...], w_ref[...])


def _inproj(x, gain, shift, scale, w):
    T, D = x.shape
    NZ = w.shape[1]
    tm = min(512, T)
    tn = 1152
    vec = pl.BlockSpec((1, D), lambda i, j: (0, 0))
    return pl.pallas_call(
        _inproj_kernel,
        grid=(T // tm, NZ // tn),
        in_specs=[pl.BlockSpec((tm, D), lambda i, j: (i, 0)), vec, vec, vec,
                  pl.BlockSpec((D, tn), lambda i, j: (0, j))],
        out_specs=pl.BlockSpec((tm, tn), lambda i, j: (i, j)),
        out_shape=jax.ShapeDtypeStruct((T, NZ), F32),
        scratch_shapes=[pltpu.VMEM((tm, D), BF16)],
        compiler_params=_params("arbitrary", "arbitrary"),
    )(x, gain, shift, scale, w)


def _gla_kernel(*refs, reverse, rope, final, tb):
    it = iter(refs)
    q_ref, k_ref, v_ref, lr_ref = next(it), next(it), next(it), next(it)
    cos_ref, sin_ref = (next(it), next(it)) if rope else (None, None)
    wg_ref, bg_ref, tri_ref, s0_ref = next(it), next(it), next(it), next(it)
    oprev_ref, r_ref, ng_ref = (next(it), next(it), next(it)) if final else (None, None, None)
    o_ref, sfin_ref, st_scr = next(it), next(it), next(it)

    @pl.when(pl.program_id(0) == 0)
    def _():
        st_scr[...] = s0_ref[...]

    q = q_ref[...]
    k = k_ref[...]
    if rope:
        lane = lax.broadcasted_iota(jnp.int32, q.shape, 1)
        first = (lane % GLA_DK) < (GLA_DK // 2)
        cos, sin = cos_ref[...], sin_ref[...]

        def rot(t):
            return jnp.where(first, pltpu.roll(t, GLA_QK - GLA_DK // 2, 1), pltpu.roll(t, GLA_DK // 2, 1))

        q = q * cos + rot(q) * sin
        k = k * cos + rot(k) * sin
    q = q * (GLA_DK ** -0.5)

    g = _log_sigmoid(_dot_hi(lr_ref[...], wg_ref[...]) + bg_ref[...]) * (1.0 / GLA_TAU)
    g_hi = g.astype(BF16)
    g_r1 = g - g_hi.astype(F32)
    g_mid = g_r1.astype(BF16)
    g_lo = (g_r1 - g_mid.astype(F32)).astype(BF16)
    tri = tri_ref[...]
    gc = _dot(tri, g_hi) + _dot(tri, g_mid) + _dot(tri, g_lo)

    C = GLA_CHUNK
    ri = lax.broadcasted_iota(jnp.int32, (C, C), 0)
    ci = lax.broadcasted_iota(jnp.int32, (C, C), 1)
    causal = (ci >= ri) if reverse else (ci <= ri)
    nchunk = tb // C
    for c in (range(nchunk - 1, -1, -1) if reverse else range(nchunk)):
        sl = slice(c * C, (c + 1) * C)
        gcc = gc[sl]
        gt = gcc[0:1] if reverse else gcc[C - 1:C]
        qe = q[sl] * jnp.exp(gcc)
        ke = k[sl] * jnp.exp(-gcc)
        kd = k[sl] * jnp.exp(gt - gcc)
        dec = jnp.exp(gt)
        for h in range(GLA_HEADS):
            hs = slice(h * GLA_DK, (h + 1) * GLA_DK)
            vs = slice(h * GLA_DV, (h + 1) * GLA_DV)
            qh = qe[:, hs].astype(BF16)
            vh = v_ref[sl, vs].astype(BF16)
            att = jnp.where(causal, _dot_nt(qh, ke[:, hs].astype(BF16)), 0.0)
            st = st_scr[h]
            o = _dot(att.astype(BF16), vh) + _dot_nt(qh, st.astype(BF16))
            st_scr[h] = st * dec[:, hs] + _dot_tn(vh, kd[:, hs].astype(BF16))
            if final:
                y = _rms(oprev_ref[sl, vs] + o) * ng_ref[...]
                rg = r_ref[sl, vs]
                o_ref[sl, vs] = (y * (rg * jax.nn.sigmoid(rg))).astype(o_ref.dtype)
            else:
                o_ref[sl, vs] = o
    sfin_ref[...] = st_scr[...]


def _gla_pass(z, cos, sin, wg, bg, tri, s0, *, reverse, oprev=None, norm_g=None):
    T = z.shape[0]
    tb = tri.shape[0]
    nb = T // tb
    rope = cos is not None
    final = oprev is not None
    blk = (lambda i: nb - 1 - i) if reverse else (lambda i: i)

    def col(width, start):
        return pl.BlockSpec((tb, width), lambda i: (blk(i), start // width))

    const2 = lambda shape: pl.BlockSpec(shape, lambda i: (0, 0))
    state = pl.BlockSpec((GLA_HEADS, GLA_DV, GLA_DK), lambda i: (0, 0, 0))
    args = [z, z, z, z]
    specs = [col(GLA_QK, Z_GQ), col(GLA_QK, Z_GK), col(GLA_V, Z_GV), col(LANES, Z_LR)]
    if rope:
        args += [cos, sin]
        specs += [col(GLA_QK, 0), col(GLA_QK, 0)]
    args += [wg, bg, tri, s0]
    specs += [const2((LANES, GLA_QK)), const2((1, GLA_QK)), const2((tb, tb)), state]
    if final:
        args += [oprev, z, norm_g]
        specs += [col(GLA_V, 0), col(GLA_V, Z_GR), const2((1, GLA_DV))]
    return pl.pallas_call(
        functools.partial(_gla_kernel, reverse=reverse, rope=rope, final=final, tb=tb),
        grid=(nb,),
        in_specs=specs,
        out_specs=[col(GLA_V, 0), state],
        out_shape=[jax.ShapeDtypeStruct((T, GLA_V), BF16 if final else F32),
                   jax.ShapeDtypeStruct((GLA_HEADS, GLA_DV, GLA_DK), F32)],
        scratch_shapes=[pltpu.VMEM((GLA_HEADS, GLA_DV, GLA_DK), F32)],
        compiler_params=_params("arbitrary"),
    )(*args)


def _tri_matrix(tb, reverse):
    i = np.arange(tb)
    same = (i[:, None] // GLA_CHUNK) == (i[None, :] // GLA_CHUNK)
    order = (i[None, :] >= i[:, None]) if reverse else (i[None, :] <= i[:, None])
    return jnp.asarray(same & order, BF16)


def _rope_tables(T):
    pos = jnp.arange(T)
    row = (pos // GRID_W).astype(F32)
    col = (pos % GRID_W).astype(F32)
    nf = GLA_DK // 4
    inv = ROPE_BASE ** (-jnp.arange(nf, dtype=F32) / nf)
    ang = jnp.concatenate([row[:, None] * inv, col[:, None] * inv], axis=-1)
    cos, sin = jnp.cos(ang), jnp.sin(ang)
    cos_full = jnp.tile(jnp.concatenate([cos, cos], axis=-1), (1, GLA_HEADS))
    sin_full = jnp.tile(jnp.concatenate([-sin, sin], axis=-1), (1, GLA_HEADS))
    return cos_full, sin_full


def _naprep_kernel(q_ref, k_ref, v_ref, qg_ref, kg_ref, qo_ref, ko_ref, vo_ref):
    for h in range(NA_HEADS):
        hs = slice(h * NA_HD, (h + 1) * NA_HD)
        qo_ref[h] = (_rms(q_ref[:, hs]) * qg_ref[...] * (NA_HD ** -0.5)).astype(BF16)
        ko_ref[h] = (_rms(k_ref[:, hs]) * kg_ref[...]).astype(BF16)
        vo_ref[h] = v_ref[:, hs].astype(BF16)


def _na_prep(z, qg, kg):
    T = z.shape[0]
    tm = min(512, T)
    col = lambda c: pl.BlockSpec((tm, NA_W), lambda i: (i, c))
    vec = pl.BlockSpec((1, NA_HD), lambda i: (0, 0))
    out = pl.BlockSpec((NA_HEADS, tm, NA_HD), lambda i: (0, i, 0))
    shp = jax.ShapeDtypeStruct((NA_HEADS, T, NA_HD), BF16)
    return pl.pallas_call(
        _naprep_kernel,
        grid=(T // tm,),
        in_specs=[col(Z_NAQ // NA_W), col(Z_NAK // NA_W), col(Z_NAV // NA_W), vec, vec],
        out_specs=[out, out, out],
        out_shape=[shp, shp, shp],
        compiler_params=_params("arbitrary"),
    )(z, z, z, qg, kg)


def _na_kernel(q_ref, k_ref, v_ref, kc_ref, vc_ref, bias_ref, o_ref, *, rpb_rows, n_rows):
    i = pl.program_id(1)
    kc = kc_ref[0]
    vc = vc_ref[0]
    W = GRID_W
    for rr in range(rpb_rows):
        r = i * rpb_rows + rr
        rs = jnp.clip(r - NA_KH // 2, 0, n_rows - NA_KH)
        off = rs - r + (NA_KH - 1)
        start = pl.multiple_of(rs * W, W)
        q = q_ref[0, rr * W:(rr + 1) * W, :]
        kw = k_ref[0, pl.ds(start, NA_KH * W), :]
        vw = v_ref[0, pl.ds(start, NA_KH * W), :]
        s_loc = _dot_nt(q, kw) + bias_ref[0, off]
        s_ctx = _dot_nt(q, kc)
        m = jnp.maximum(jnp.max(s_loc, axis=-1, keepdims=True), jnp.max(s_ctx, axis=-1, keepdims=True))
        p_loc = jnp.exp(s_loc - m)
        p_ctx = jnp.exp(s_ctx - m)
        den = jnp.sum(p_loc, axis=-1, keepdims=True) + jnp.sum(p_ctx, axis=-1, keepdims=True)
        o = _dot(p_loc.astype(BF16), vw) + _dot(p_ctx.astype(BF16), vc)
        o_ref[rr * W:(rr + 1) * W, :] = (o / den).astype(BF16)


def _na_latent(q, k, v, kc, vc, bias):
    H, T, hd = q.shape
    Lc = kc.shape[1]
    n_rows = T // GRID_W
    R = 4
    whole = lambda n: pl.BlockSpec((1, n, hd), lambda h, i: (h, 0, 0))
    return pl.pallas_call(
        functools.partial(_na_kernel, rpb_rows=R, n_rows=n_rows),
        grid=(H, n_rows // R),
        in_specs=[pl.BlockSpec((1, R * GRID_W, hd), lambda h, i: (h, i, 0)),
                  whole(T), whole(T), whole(Lc), whole(Lc),
                  pl.BlockSpec((1, NA_KH, GRID_W, NA_KH * GRID_W), lambda h, i: (h, 0, 0, 0))],
        out_specs=pl.BlockSpec((R * GRID_W, hd), lambda h, i: (i, h)),
        out_shape=jax.ShapeDtypeStruct((T, H * hd), BF16),
        compiler_params=_params("arbitrary", "arbitrary"),
    )(q, k, v, kc, vc, bias)


def _ctxattn_kernel(q_ref, k_ref, v_ref, o_ref):
    s = _dot_nt(q_ref[0], k_ref[0])
    p = jnp.exp(s - jnp.max(s, axis=-1, keepdims=True))
    o = _dot(p.astype(BF16), v_ref[0]) / jnp.sum(p, axis=-1, keepdims=True)
    o_ref[...] = o.astype(BF16)


def _na_context(q, k, v):
    H, Lc, hd = q.shape
    spec = pl.BlockSpec((1, Lc, hd), lambda h: (h, 0, 0))
    return pl.pallas_call(
        _ctxattn_kernel,
        grid=(H,),
        in_specs=[spec, spec, spec],
        out_specs=pl.BlockSpec((Lc, hd), lambda h: (0, h)),
        out_shape=jax.ShapeDtypeStruct((Lc, H * hd), BF16),
        compiler_params=_params("arbitrary"),
    )(q, k, v)


def _na_bias(rpb):
    cols = np.arange(GRID_W)
    col_start = np.clip(cols - NA_KW // 2, 0, GRID_W - NA_KW)
    valid = (cols[None, :] >= col_start[:, None]) & (cols[None, :] < col_start[:, None] + NA_KW)
    dc = np.clip(cols[None, :] - cols[:, None] + (NA_KW - 1), 0, 2 * NA_KW - 2)
    tab = rpb[:, :, dc]
    rows = np.arange(NA_KH)[:, None] + np.arange(NA_KH)[None, :]
    slab = tab[:, rows]
    slab = jnp.where(valid[None, None, None], slab, NEG)
    slab = slab.transpose(0, 1, 3, 2, 4)
    return slab.reshape(rpb.shape[0], NA_KH, GRID_W, NA_KH * GRID_W).astype(F32)


def _lru_kernel(*refs, reverse, final, tb, nblk):
    it = iter(refs)
    x_ref, xp_ref, xn_ref = next(it), next(it), next(it)
    cw_ref, cb_ref, wa_ref, ba_ref, wi_ref, bi_ref, lam_ref, h0_ref = (next(it) for _ in range(8))
    hprev_ref, y_ref = (next(it), next(it)) if final else (None, None)
    o_ref, hfin_ref = next(it), next(it)
    xx_scr, a_scr, b_scr, carry_scr = next(it), next(it), next(it), next(it)

    i = pl.program_id(0)
    blk = (nblk - 1 - i) if reverse else i
    S = SUBLANES
    Wd = LRU_WIDTH

    @pl.when(i == 0)
    def _():
        carry_scr[...] = jnp.broadcast_to(h0_ref[...], (S, Wd))

    xx_scr[0:S] = jnp.where(blk > 0, xp_ref[...], 0.0)
    xx_scr[S:S + tb] = x_ref[...]
    xx_scr[S + tb:2 * S + tb] = jnp.where(blk < nblk - 1, xn_ref[...], 0.0)
    xc = cb_ref[...]
    for j in range(LRU_CONV):
        xc = xc + cw_ref[j:j + 1] * xx_scr[S - 1 + j:S - 1 + j + tb]

    bw = Wd // LRU_BLOCKS
    ra, ri = [], []
    for n in range(LRU_BLOCKS):
        xb = xc[:, n * bw:(n + 1) * bw].astype(BF16)
        ra.append(_dot(xb, wa_ref[n]))
        ri.append(_dot(xb, wi_ref[n]))
    rg = jax.nn.sigmoid(jnp.concatenate(ra, axis=1) + ba_ref[...])
    ig = jax.nn.sigmoid(jnp.concatenate(ri, axis=1) + bi_ref[...])
    log_a = LRU_C * rg * _log_sigmoid(lam_ref[...])
    a = jnp.exp(log_a)
    a_scr[...] = a
    b_scr[...] = jnp.sqrt(-jnp.tanh(log_a) * (a * a + 1.0)) * (ig * xc)

    row = lax.broadcasted_iota(jnp.int32, (S, Wd), 0)
    ng = tb // S

    def body(gi, carry):
        g = (ng - 1 - gi) if reverse else gi
        off = pl.multiple_of(g * S, S)
        a = a_scr[pl.ds(off, S), :]
        b = b_scr[pl.ds(off, S), :]
        for s in (1, 2, 4):
            sh = (S - s) if reverse else s
            keep = (row < S - s) if reverse else (row >= s)
            b = jnp.where(keep, a * pltpu.roll(b, sh, 0) + b, b)
            a = jnp.where(keep, a * pltpu.roll(a, sh, 0), a)
        h = b + a * carry
        b_scr[pl.ds(off, S), :] = h
        return jnp.broadcast_to(h[0:1] if reverse else h[S - 1:S], (S, Wd))

    carry = lax.fori_loop(0, ng, body, carry_scr[...])
    carry_scr[...] = carry
    hfin_ref[...] = carry
    if final:
        o_ref[...] = ((hprev_ref[...] + b_scr[...]) * jax.nn.gelu(y_ref[...])).astype(o_ref.dtype)
    else:
        o_ref[...] = b_scr[...]


def _lru_pass(z, p, h0, *, reverse, hprev=None):
    T = z.shape[0]
    tb = min(512, T)
    nb = T // tb
    S = SUBLANES
    Wd = LRU_WIDTH
    final = hprev is not None
    blk = (lambda i: nb - 1 - i) if reverse else (lambda i: i)
    gpb = tb // S
    xcol = Z_LX // Wd
    row = pl.BlockSpec((1, Wd), lambda i: (0, 0))
    wspec = pl.BlockSpec((LRU_BLOCKS, Wd // LRU_BLOCKS, Wd // LRU_BLOCKS), lambda i: (0, 0, 0))
    args = [z, z, z, p["conv_w"], p["conv_b"], p["w_a"], p["b_a"], p["w_i"], p["b_i"], p["lam"], h0]
    specs = [pl.BlockSpec((tb, Wd), lambda i: (blk(i), xcol)),
             pl.BlockSpec((S, Wd), lambda i: (jnp.maximum(blk(i) * gpb - 1, 0), xcol)),
             pl.BlockSpec((S, Wd), lambda i: (jnp.minimum((blk(i) + 1) * gpb, T // S - 1), xcol)),
             pl.BlockSpec((LRU_CONV, Wd), lambda i: (0, 0)), row, wspec, row, wspec, row, row, row]
    if final:
        args += [hprev, z]
        specs += [pl.BlockSpec((tb, Wd), lambda i: (blk(i), 0)),
                  pl.BlockSpec((tb, Wd), lambda i: (blk(i), Z_LY // Wd))]
    return pl.pallas_call(
        functools.partial(_lru_kernel, reverse=reverse, final=final, tb=tb, nblk=nb),
        grid=(nb,),
        in_specs=specs,
        out_specs=[pl.BlockSpec((tb, Wd), lambda i: (blk(i), 0)), pl.BlockSpec((S, Wd), lambda i: (0, 0))],
        out_shape=[jax.ShapeDtypeStruct((T, Wd), BF16 if final else F32),
                   jax.ShapeDtypeStruct((S, Wd), F32)],
        scratch_shapes=[pltpu.VMEM((tb + 2 * S, Wd), F32), pltpu.VMEM((tb, Wd), F32),
                        pltpu.VMEM((tb, Wd), F32), pltpu.VMEM((S, Wd), F32)],
        compiler_params=_params("arbitrary"),
    )(*args)


def _outproj_kernel(gla_ref, na_ref, lru_ref, x_ref, wg_ref, wn_ref, wl_ref, gate_ref, g2_ref, sh_ref, sc_ref,
                    wr_ref, xo_ref, h_ref, aff_ref):
    acc = _dot(gla_ref[...], wg_ref[...]) + _dot(na_ref[...], wn_ref[...]) + _dot(lru_ref[...], wl_ref[...])
    xn = x_ref[...] + gate_ref[...] * acc
    xo_ref[...] = xn
    h = (_rms(xn) * g2_ref[...]) * (1.0 + sc_ref[...]) + sh_ref[...]
    h_ref[...] = h.astype(BF16)
    logits = _dot_hi(h, wr_ref[...])
    lane = lax.broadcasted_iota(jnp.int32, logits.shape, 1)
    logits = jnp.where(lane < N_EXPERTS, logits, NEG)
    e = jnp.exp(logits - jnp.max(logits, axis=-1, keepdims=True))
    aff_ref[...] = e / jnp.sum(e, axis=-1, keepdims=True)


def _outproj(gla, na, lru, x, wg, wn, wl, gate, g2, shift, scale, wr):
    T, D = x.shape
    tm = min(256, T)
    rows = lambda w: pl.BlockSpec((tm, w), lambda i: (i, 0))
    full = lambda a: pl.BlockSpec(a.shape, lambda i: (0, 0))
    vec = pl.BlockSpec((1, D), lambda i: (0, 0))
    return pl.pallas_call(
        _outproj_kernel,
        grid=(T // tm,),
        in_specs=[rows(GLA_V), rows(NA_W), rows(LRU_WIDTH), rows(D), full(wg), full(wn), full(wl),
                  vec, vec, vec, vec, full(wr)],
        out_specs=[rows(D), rows(D), rows(LANES)],
        out_shape=[jax.ShapeDtypeStruct((T, D), F32), jax.ShapeDtypeStruct((T, D), BF16),
                   jax.ShapeDtypeStruct((T, LANES), F32)],
        compiler_params=_params("arbitrary"),
    )(gla, na, lru, x, wg, wn, wl, gate, g2, shift, scale, wr)


def _gather_kernel(idx_ref, h_hbm, o_ref, sem, *, rt):
    e = pl.program_id(0)
    t = pl.program_id(1)

    def issue(r, carry):
        tok = idx_ref[e, t * rt + r]
        pltpu.make_async_copy(h_hbm.at[tok], o_ref.at[0, r], sem).start()
        return carry

    lax.fori_loop(0, rt, issue, 0)
    pltpu.make_async_copy(h_hbm.at[pl.ds(0, rt)], o_ref.at[0], sem).wait()


def _gather_rows(h, idx):
    T, D = h.shape
    E, cap = idx.shape
    rt = min(256, cap)
    S = D // LANES
    out = pl.pallas_call(
        functools.partial(_gather_kernel, rt=rt),
        grid_spec=pltpu.PrefetchScalarGridSpec(
            num_scalar_prefetch=1,
            grid=(E, cap // rt),
            in_specs=[pl.BlockSpec(memory_space=pl.ANY)],
            out_specs=pl.BlockSpec((1, rt, S, LANES), lambda e, t, idx: (e, t, 0, 0)),
            scratch_shapes=[pltpu.SemaphoreType.DMA(())]),
        out_shape=jax.ShapeDtypeStruct((E, cap, S, LANES), BF16),
        compiler_params=_params("arbitrary", "arbitrary"),
    )(idx, h.reshape(T, S, LANES))
    return out.reshape(E, cap, D)


def _moe_kernel(*refs, nj, has_ctx):
    it = iter(refs)
    xs_ref, tw_ref, gate_ref = next(it), next(it), next(it)
    xc_ref, twc_ref, gatec_ref = (next(it), next(it), next(it)) if has_ctx else (None, None, None)
    wg_ref, wu_ref, wd_ref = next(it), next(it), next(it)
    ye_ref = next(it)
    yc_ref = next(it) if has_ctx else None
    half = pl.program_id(1)
    j = pl.program_id(2)
    wg = wg_ref[0].astype(BF16)
    wu = wu_ref[0].astype(BF16)
    wd = wd_ref[0].astype(BF16)

    def accumulate(x_ref, y_ref, w_ref, g_ref):
        x = x_ref[0]
        a = _dot(x, wg)
        hid = (a * jax.nn.sigmoid(a)) * _dot(x, wu)
        part = _dot(hid.astype(BF16), wd)

        @pl.when(j == 0)
        def _():
            y_ref[0] = part

        @pl.when(j > 0)
        def _():
            y_ref[0] += part

        @pl.when(j == nj - 1)
        def _():
            y_ref[0] = y_ref[0] * w_ref[0] * g_ref[...]

    accumulate(xs_ref, ye_ref, tw_ref, gate_ref)
    if has_ctx:
        @pl.when(half == 0)
        def _():
            accumulate(xc_ref, yc_ref, twc_ref, gatec_ref)


def _moe(xs, tw, gate, w_gate, w_up, w_down, ctx=None):
    E, cap, D = xs.shape
    FF = w_gate.shape[2]
    nh = 2 if cap % 32 == 0 and cap >= 512 else 1
    mh = cap // nh
    tf = 256
    nj = FF // tf
    has_ctx = ctx is not None
    vec = pl.BlockSpec((1, D), lambda e, h, j: (0, 0))
    args = [xs, tw, gate]
    specs = [pl.BlockSpec((1, mh, D), lambda e, h, j: (e, h, 0)),
             pl.BlockSpec((1, mh, 1), lambda e, h, j: (e, h, 0)), vec]
    outs = [pl.BlockSpec((1, mh, D), lambda e, h, j: (e, h, 0))]
    shapes = [jax.ShapeDtypeStruct((E, cap, D), F32)]
    if has_ctx:
        xc, twc, gatec = ctx
        cc = xc.shape[1]
        args += [xc, twc, gatec]
        specs += [pl.BlockSpec((1, cc, D), lambda e, h, j: (e, 0, 0)),
                  pl.BlockSpec((1, cc, 1), lambda e, h, j: (e, 0, 0)), vec]
        outs.append(pl.BlockSpec((1, cc, D), lambda e, h, j: (e, 0, 0)))
        shapes.append(jax.ShapeDtypeStruct((E, cc, D), F32))
    args += [w_gate, w_up, w_down]
    specs += [pl.BlockSpec((1, D, tf), lambda e, h, j: (e, 0, j)),
              pl.BlockSpec((1, D, tf), lambda e, h, j: (e, 0, j)),
              pl.BlockSpec((1, tf, D), lambda e, h, j: (e, j, 0))]
    res = pl.pallas_call(
        functools.partial(_moe_kernel, nj=nj, has_ctx=has_ctx),
        grid=(E, nh, nj),
        in_specs=specs,
        out_specs=outs,
        out_shape=shapes,
        compiler_params=_params("arbitrary", "arbitrary", "arbitrary"),
    )(*args)
    return res if has_ctx else (res[0], None)


def _route(aff, h, n_tok):
    cap = CAPACITY * n_tok // N_EXPERTS
    top_w, top_idx = lax.top_k(aff[:, :N_EXPERTS].T, cap)
    return _gather_rows(h, top_idx), top_w[..., None], top_idx


def _reorder_w_in(w):
    D = w.shape[0]
    sizes = (GLA_QK, GLA_QK, GLA_V, GLA_V, GLA_GATE_RANK, GLA_GATE_RANK, NA_W, NA_W, NA_W, LRU_WIDTH, LRU_WIDTH)
    parts, o = [], 0
    for s in sizes:
        parts.append(w[:, o:o + s])
        o += s
    gq, gk, gv, gr, lf, lb, nq, nk, nv, lx, ly = parts
    pad = jnp.zeros((D, LANES - 2 * GLA_GATE_RANK), w.dtype)
    return jnp.concatenate([nq, nk, nv, gv, gr, lx, ly, gq, gk, lf, lb, pad], axis=1).astype(BF16)


def kernel(x, c, ctx, c_ctx, norm1_g, norm2_g, w_mod, b_mod, w_in, gla_w_gate, gla_b_gate, gla_norm_g, na_q_norm_g, na_k_norm_g, na_rpb, lru_conv_w, lru_conv_b, lru_w_a, lru_b_a, lru_w_i, lru_b_i, lru_lambda, w_out, w_router, w_exp_gate, w_exp_up, w_exp_down):
    B, T, D = x.shape
    Lc = ctx.shape[1]
    depth = w_in.shape[0]
    assert B == 1 and T % (4 * GRID_W) == 0 and T // GRID_W >= NA_KH
    xs = x[0]
    cs = ctx[0]
    mod = _modulation(jnp.stack([c[0], c_ctx]), w_mod, b_mod)
    cos, sin = _rope_tables(T)
    tb_l, tb_c = min(512, T), min(512, Lc)
    tri = {(tb, rev): _tri_matrix(tb, rev) for tb in {tb_l, tb_c} for rev in (False, True)}
    row = lambda v: v.reshape(1, -1)

    for l in range(depth):
        ctx_out = l < depth - 1
        mx = [row(m) for m in jnp.split(mod[l, 0], 6)]
        mc = [row(m) for m in jnp.split(mod[l, 1], 6)]
        w_in_l = _reorder_w_in(w_in[l])
        g1 = row(norm1_g[l])
        zx = _inproj(xs, g1, mx[0], mx[1], w_in_l)
        zc = _inproj(cs, g1, mc[0], mc[1], w_in_l)

        gla_dir = []
        for d in range(2):
            wg = jnp.zeros((LANES, GLA_QK), F32).at[d * GLA_GATE_RANK:(d + 1) * GLA_GATE_RANK].set(gla_w_gate[l, d])
            gla_dir.append((wg, row(gla_b_gate[l, d])))
        ng = row(gla_norm_g[l])
        zero_state = jnp.zeros((GLA_HEADS, GLA_DV, GLA_DK), F32)
        oc_f, sc_f = _gla_pass(zc, None, None, *gla_dir[0], tri[(tb_c, False)], zero_state, reverse=False)
        gla_c, sc_b = _gla_pass(zc, None, None, *gla_dir[1], tri[(tb_c, True)], zero_state, reverse=True,
                                oprev=oc_f, norm_g=ng)
        ol_f, _ = _gla_pass(zx, cos, sin, *gla_dir[0], tri[(tb_l, False)], sc_f, reverse=False)
        gla_x, _ = _gla_pass(zx, cos, sin, *gla_dir[1], tri[(tb_l, True)], sc_b, reverse=True,
                             oprev=ol_f, norm_g=ng)

        qg, kg = row(na_q_norm_g[l]), row(na_k_norm_g[l])
        ql, kl, vl = _na_prep(zx, qg, kg)
        qc, kc, vc = _na_prep(zc, qg, kg)
        na_x = _na_latent(ql, kl, vl, kc, vc, _na_bias(na_rpb[l]))

        lru_dir = []
        for d in range(2):
            lru_dir.append(dict(conv_w=lru_conv_w[l], conv_b=row(lru_conv_b[l]),
                                w_a=lru_w_a[l, d].astype(BF16), b_a=row(lru_b_a[l, d]),
                                w_i=lru_w_i[l, d].astype(BF16), b_i=row(lru_b_i[l, d]),
                                lam=row(lru_lambda[l, d])))
        zero_h = jnp.zeros((1, LRU_WIDTH), F32)
        hc_f, fin_f = _lru_pass(zc, lru_dir[0], zero_h, reverse=False)
        lru_c, fin_b = _lru_pass(zc, lru_dir[1], zero_h, reverse=True, hprev=hc_f)
        hl_f, _ = _lru_pass(zx, lru_dir[0], fin_f[0:1], reverse=False)
        lru_x, _ = _lru_pass(zx, lru_dir[1], fin_b[0:1], reverse=True, hprev=hl_f)

        wo = w_out[l].astype(BF16)
        wo_g, wo_n, wo_l = wo[:GLA_V], wo[GLA_V:GLA_V + NA_W], wo[GLA_V + NA_W:]
        wr = jnp.zeros((D, LANES), F32).at[:, :N_EXPERTS].set(w_router[l])
        g2 = row(norm2_g[l])
        x1, hx, affx = _outproj(gla_x, na_x, lru_x, xs, wo_g, wo_n, wo_l, mx[2], g2, mx[3], mx[4], wr)
        xg, twx, idxx = _route(affx, hx, T)
        if ctx_out:
            na_c = _na_context(qc, kc, vc)
            c1, hc, affc = _outproj(gla_c, na_c, lru_c, cs, wo_g, wo_n, wo_l, mc[2], g2, mc[3], mc[4], wr)
            cg, twc, idxc = _route(affc, hc, Lc)
            ye, yc = _moe(xg, twx, mx[5], w_exp_gate[l], w_exp_up[l], w_exp_down[l], ctx=(cg, twc, mc[5]))
            cs = c1.at[idxc.reshape(-1)].add(yc.reshape(-1, D))
        else:
            ye, _ = _moe(xg, twx, mx[5], w_exp_gate[l], w_exp_up[l], w_exp_down[l])
        xs = x1.at[idxx.reshape(-1)].add(ye.reshape(-1, D))
    return xs[None]
```

```python
import functools

import numpy as np
import jax
import jax.numpy as jnp
from jax import lax
from jax.experimental import pallas as pl
from jax.experimental.pallas import tpu as pltpu

F32 = jnp.float32
BF16 = jnp.bfloat16

GRID_W = 64
EPS = 1e-6
GLA_HEADS = 4
GLA_DK = 64
GLA_DV = 128
GLA_GATE_RANK = 16
GLA_TAU = 16.0
GLA_CHUNK = 64
ROPE_BASE = 10000.0
NA_HEADS = 8
NA_HD = 128
NA_KH = 8
NA_KW = 16
LRU_WIDTH = 512
LRU_BLOCKS = 4
LRU_CONV = 4
LRU_C = 8.0
N_EXPERTS = 16
CAPACITY = 2

GLA_QK = GLA_HEADS * GLA_DK
GLA_V = GLA_HEADS * GLA_DV
NA_W = NA_HEADS * NA_HD

LANES = 128
SUBLANES = 8
VMEM_LIMIT = 56 * 1024 * 1024
NEG = -1e30
ROW_TILE = 256
ISSUE_UNROLL = 8
MOE_TILE = 256
MOE_ROW_CHUNK = 512
LRU_HALO = 16
NA_ROWS_PER_STEP = 8

Z_NAQ, Z_NAK, Z_NAV = 0, NA_W, 2 * NA_W
Z_GV = 3 * NA_W
Z_GR = Z_GV + GLA_V
Z_LX = Z_GR + GLA_V
Z_LY = Z_LX + LRU_WIDTH
Z_GQ = Z_LY + LRU_WIDTH
Z_GK = Z_GQ + GLA_QK
Z_COLS = Z_GK + GLA_QK
Z_CHUNK = 512


def _params(*sem):
    return pltpu.CompilerParams(dimension_semantics=sem, vmem_limit_bytes=VMEM_LIMIT)


def _dot(a, b):
    return jnp.dot(a, b, preferred_element_type=F32)


def _dot_nt(a, b):
    return lax.dot_general(a, b, (((1,), (1,)), ((), ())), preferred_element_type=F32)


def _dot_tn(a, b):
    return lax.dot_general(a, b, (((0,), (0,)), ((), ())), preferred_element_type=F32)


def _split2(a):
    hi = a.astype(BF16)
    lo = (a - hi.astype(F32)).astype(BF16)
    return hi, lo


def _dot_hi(a, b):
    ah, al = _split2(a)
    bh, bl = _split2(b)
    return _dot(ah, bh) + _dot(al, bh) + _dot(ah, bl)


def _rms(x):
    return x * lax.rsqrt(jnp.mean(x * x, axis=-1, keepdims=True) + EPS)


def _norm2(x, g_ref, sh_ref, sc_ref):
    return (_rms(x) * g_ref[...]) * (1.0 + sc_ref[...]) + sh_ref[...]


def _log_sigmoid(z):
    return jnp.minimum(z, 0.0) - jnp.log1p(jnp.exp(-jnp.abs(z)))


def _mod_kernel(cb_ref, w_ref, b_ref, o_ref):
    tn = w_ref.shape[2]
    s = [cb_ref[v] * jax.nn.sigmoid(cb_ref[v]) for v in range(2)]
    for j in range(tn // LANES):
        cs = slice(j * LANES, (j + 1) * LANES)
        wj = w_ref[0, :, cs]
        for v in range(2):
            o_ref[0, v:v + 1, cs] = jnp.sum(wj * s[v], axis=0, keepdims=True) + b_ref[0, :, cs]


def _modulation(c2, w_mod, b_mod):
    L, D, N6 = w_mod.shape
    tn = 1024
    cb = jnp.broadcast_to(c2[:, :, None], (2, D, LANES))
    return pl.pallas_call(
        _mod_kernel,
        grid=(L, N6 // tn),
        in_specs=[pl.BlockSpec((2, D, LANES), lambda l, j: (0, 0, 0)),
                  pl.BlockSpec((1, D, tn), lambda l, j: (l, 0, j)),
                  pl.BlockSpec((1, 1, tn), lambda l, j: (l, 0, j))],
        out_specs=pl.BlockSpec((1, 2, tn), lambda l, j: (l, 0, j)),
        out_shape=jax.ShapeDtypeStruct((L, 2, N6), F32),
        compiler_params=_params("arbitrary", "arbitrary"),
    )(cb, w_mod, b_mod.reshape(L, 1, N6))


def _inproj_kernel(x_ref, g_ref, sh_ref, sc_ref, w_ref, wlr_ref, o_ref, olr_ref):
    h = _norm2(x_ref[...], g_ref, sh_ref, sc_ref).astype(BF16)
    for n in range(w_ref.shape[1] // Z_CHUNK):
        cs = slice(n * Z_CHUNK, (n + 1) * Z_CHUNK)
        o_ref[:, cs] = _dot(h, w_ref[:, cs]).astype(BF16)
    olr_ref[...] = _dot(h, wlr_ref[...])


def _inproj(x, gain, shift, scale, w, wlr):
    T, D = x.shape
    NZ = w.shape[1]
    tm = min(512, T)
    vec = pl.BlockSpec((1, D), lambda i: (0, 0))
    resident = lambda a: pl.BlockSpec(a.shape, lambda i: (0, 0), pipeline_mode=pl.Buffered(1))
    return pl.pallas_call(
        _inproj_kernel,
        grid=(T // tm,),
        in_specs=[pl.BlockSpec((tm, D), lambda i: (i, 0)), vec, vec, vec, resident(w), resident(wlr)],
        out_specs=[pl.BlockSpec((tm, NZ), lambda i: (i, 0)), pl.BlockSpec((tm, LANES), lambda i: (i, 0))],
        out_shape=[jax.ShapeDtypeStruct((T, NZ), BF16), jax.ShapeDtypeStruct((T, LANES), F32)],
        compiler_params=_params("arbitrary"),
    )(x, gain, shift, scale, w, wlr)


def _gla_kernel(*refs, reverse, rope, final, tb):
    it = iter(refs)
    q_ref, k_ref, v_ref, lr_ref = next(it), next(it), next(it), next(it)
    cos_ref, sin_ref = (next(it), next(it)) if rope else (None, None)
    wg_ref, bg_ref, tri_ref, s0_ref = next(it), next(it), next(it), next(it)
    oprev_ref, r_ref, ng_ref = (next(it), next(it), next(it)) if final else (None, None, None)
    o_ref, sfin_ref, st_scr = next(it), next(it), next(it)

    @pl.when(pl.program_id(0) == 0)
    def _():
        st_scr[...] = s0_ref[...]

    q = q_ref[...].astype(F32)
    k = k_ref[...].astype(F32)
    if rope:
        lane = lax.broadcasted_iota(jnp.int32, q.shape, 1)
        first = (lane % GLA_DK) < (GLA_DK // 2)
        cos, sin = cos_ref[...], sin_ref[...]

        def rot(t):
            return jnp.where(first, pltpu.roll(t, GLA_QK - GLA_DK // 2, 1), pltpu.roll(t, GLA_DK // 2, 1))

        q = q * cos + rot(q) * sin
        k = k * cos + rot(k) * sin
    q = q * (GLA_DK ** -0.5)

    g = _log_sigmoid(_dot_hi(lr_ref[...], wg_ref[...]) + bg_ref[...]) * (1.0 / GLA_TAU)
    g_hi = g.astype(BF16)
    g_r1 = g - g_hi.astype(F32)
    g_mid = g_r1.astype(BF16)
    g_lo = (g_r1 - g_mid.astype(F32)).astype(BF16)
    tri = tri_ref[...]
    gc = _dot(tri, g_hi) + _dot(tri, g_mid) + _dot(tri, g_lo)

    C = GLA_CHUNK
    ri = lax.broadcasted_iota(jnp.int32, (C, C), 0)
    ci = lax.broadcasted_iota(jnp.int32, (C, C), 1)
    causal = (ci >= ri) if reverse else (ci <= ri)
    nchunk = tb // C
    for c in (range(nchunk - 1, -1, -1) if reverse else range(nchunk)):
        sl = slice(c * C, (c + 1) * C)
        gcc = gc[sl]
        gt = gcc[0:1] if reverse else gcc[C - 1:C]
        qe = q[sl] * jnp.exp(gcc)
        ke = k[sl] * jnp.exp(-gcc)
        kd = k[sl] * jnp.exp(gt - gcc)
        dec = jnp.exp(gt)
        for h in range(GLA_HEADS):
            hs = slice(h * GLA_DK, (h + 1) * GLA_DK)
            vs = slice(h * GLA_DV, (h + 1) * GLA_DV)
            qh = qe[:, hs].astype(BF16)
            vh = v_ref[sl, vs]
            att = jnp.where(causal, _dot_nt(qh, ke[:, hs].astype(BF16)), 0.0)
            st = st_scr[h]
            o = _dot(att.astype(BF16), vh) + _dot_nt(qh, st.astype(BF16))
            st_scr[h] = st * dec[:, hs] + _dot_tn(vh, kd[:, hs].astype(BF16))
            if final:
                y = _rms(oprev_ref[sl, vs] + o) * ng_ref[...]
                rg = r_ref[sl, vs].astype(F32)
                o_ref[sl, vs] = (y * (rg * jax.nn.sigmoid(rg))).astype(o_ref.dtype)
            else:
                o_ref[sl, vs] = o
    sfin_ref[...] = st_scr[...]


def _gla_pass(z, zlr, cos, sin, wg, bg, tri, s0, *, reverse, oprev=None, norm_g=None):
    T = z.shape[0]
    tb = tri.shape[0]
    nb = T // tb
    rope = cos is not None
    final = oprev is not None
    blk = (lambda i: nb - 1 - i) if reverse else (lambda i: i)

    def col(width, start):
        return pl.BlockSpec((tb, width), lambda i: (blk(i), start // width))

    const2 = lambda shape: pl.BlockSpec(shape, lambda i: (0, 0))
    state = pl.BlockSpec((GLA_HEADS, GLA_DV, GLA_DK), lambda i: (0, 0, 0))
    args = [z, z, z, zlr]
    specs = [col(GLA_QK, Z_GQ), col(GLA_QK, Z_GK), col(GLA_V, Z_GV), col(LANES, 0)]
    if rope:
        args += [cos, sin]
        specs += [col(GLA_QK, 0), col(GLA_QK, 0)]
    args += [wg, bg, tri, s0]
    specs += [const2((LANES, GLA_QK)), const2((1, GLA_QK)), const2((tb, tb)), state]
    if final:
        args += [oprev, z, norm_g]
        specs += [col(GLA_V, 0), col(GLA_V, Z_GR), const2((1, GLA_DV))]
    return pl.pallas_call(
        functools.partial(_gla_kernel, reverse=reverse, rope=rope, final=final, tb=tb),
        grid=(nb,),
        in_specs=specs,
        out_specs=[col(GLA_V, 0), state],
        out_shape=[jax.ShapeDtypeStruct((T, GLA_V), BF16 if final else F32),
                   jax.ShapeDtypeStruct((GLA_HEADS, GLA_DV, GLA_DK), F32)],
        scratch_shapes=[pltpu.VMEM((GLA_HEADS, GLA_DV, GLA_DK), F32)],
        compiler_params=_params("arbitrary"),
    )(*args)


def _tri_matrix(tb, reverse):
    i = np.arange(tb)
    same = (i[:, None] // GLA_CHUNK) == (i[None, :] // GLA_CHUNK)
    order = (i[None, :] >= i[:, None]) if reverse else (i[None, :] <= i[:, None])
    return jnp.asarray(same & order, BF16)


def _rope_tables(T):
    pos = jnp.arange(T)
    row = (pos // GRID_W).astype(F32)
    col = (pos % GRID_W).astype(F32)
    nf = GLA_DK // 4
    inv = ROPE_BASE ** (-jnp.arange(nf, dtype=F32) / nf)
    ang = jnp.concatenate([row[:, None] * inv, col[:, None] * inv], axis=-1)
    cos, sin = jnp.cos(ang), jnp.sin(ang)
    cos_full = jnp.tile(jnp.concatenate([cos, cos], axis=-1), (1, GLA_HEADS))
    sin_full = jnp.tile(jnp.concatenate([-sin, sin], axis=-1), (1, GLA_HEADS))
    return cos_full, sin_full


def _naprep_kernel(q_ref, k_ref, v_ref, qg_ref, kg_ref, qo_ref, ko_ref, vo_ref):
    for h in range(NA_HEADS):
        hs = slice(h * NA_HD, (h + 1) * NA_HD)
        qo_ref[h] = (_rms(q_ref[:, hs].astype(F32)) * qg_ref[...] * (NA_HD ** -0.5)).astype(BF16)
        ko_ref[h] = (_rms(k_ref[:, hs].astype(F32)) * kg_ref[...]).astype(BF16)
        vo_ref[h] = v_ref[:, hs]


def _na_prep(z, qg, kg):
    T = z.shape[0]
    tm = min(512, T)
    col = lambda c: pl.BlockSpec((tm, NA_W), lambda i: (i, c))
    vec = pl.BlockSpec((1, NA_HD), lambda i: (0, 0))
    out = pl.BlockSpec((NA_HEADS, tm, NA_HD), lambda i: (0, i, 0))
    shp = jax.ShapeDtypeStruct((NA_HEADS, T, NA_HD), BF16)
    return pl.pallas_call(
        _naprep_kernel,
        grid=(T // tm,),
        in_specs=[col(Z_NAQ // NA_W), col(Z_NAK // NA_W), col(Z_NAV // NA_W), vec, vec],
        out_specs=[out, out, out],
        out_shape=[shp, shp, shp],
        compiler_params=_params("arbitrary"),
    )(z, z, z, qg, kg)


def _na_kernel(q_ref, k_ref, v_ref, kc_ref, vc_ref, bias_ref, o_ref, *, rpb_rows, n_rows):
    i = pl.program_id(1)
    kc = kc_ref[0]
    vc = vc_ref[0]
    W = GRID_W
    for rr in range(rpb_rows):
        r = i * rpb_rows + rr
        rs = jnp.clip(r - NA_KH // 2, 0, n_rows - NA_KH)
        off = rs - r + (NA_KH - 1)
        start = pl.multiple_of(rs * W, W)
        q = q_ref[0, rr * W:(rr + 1) * W, :]
        kw = k_ref[0, pl.ds(start, NA_KH * W), :]
        vw = v_ref[0, pl.ds(start, NA_KH * W), :]
        s_loc = _dot_nt(q, kw) + bias_ref[0, off]
        s_ctx = _dot_nt(q, kc)
        m = jnp.maximum(jnp.max(s_loc, axis=-1, keepdims=True), jnp.max(s_ctx, axis=-1, keepdims=True))
        p_loc = jnp.exp(s_loc - m)
        p_ctx = jnp.exp(s_ctx - m)
        den = jnp.sum(p_loc, axis=-1, keepdims=True) + jnp.sum(p_ctx, axis=-1, keepdims=True)
        o = _dot(p_loc.astype(BF16), vw) + _dot(p_ctx.astype(BF16), vc)
        o_ref[rr * W:(rr + 1) * W, :] = (o / den).astype(BF16)


def _na_latent(q, k, v, kc, vc, bias):
    H, T, hd = q.shape
    Lc = kc.shape[1]
    n_rows = T // GRID_W
    R = NA_ROWS_PER_STEP
    whole = lambda n: pl.BlockSpec((1, n, hd), lambda h, i: (h, 0, 0))
    return pl.pallas_call(
        functools.partial(_na_kernel, rpb_rows=R, n_rows=n_rows),
        grid=(H, n_rows // R),
        in_specs=[pl.BlockSpec((1, R * GRID_W, hd), lambda h, i: (h, i, 0)),
                  whole(T), whole(T), whole(Lc), whole(Lc),
                  pl.BlockSpec((1, NA_KH, GRID_W, NA_KH * GRID_W), lambda h, i: (h, 0, 0, 0))],
        out_specs=pl.BlockSpec((R * GRID_W, hd), lambda h, i: (i, h)),
        out_shape=jax.ShapeDtypeStruct((T, H * hd), BF16),
        compiler_params=_params("arbitrary", "arbitrary"),
    )(q, k, v, kc, vc, bias)


def _ctxattn_kernel(q_ref, k_ref, v_ref, o_ref):
    s = _dot_nt(q_ref[0], k_ref[0])
    p = jnp.exp(s - jnp.max(s, axis=-1, keepdims=True))
    o = _dot(p.astype(BF16), v_ref[0]) / jnp.sum(p, axis=-1, keepdims=True)
    o_ref[...] = o.astype(BF16)


def _na_context(q, k, v):
    H, Lc, hd = q.shape
    spec = pl.BlockSpec((1, Lc, hd), lambda h: (h, 0, 0))
    return pl.pallas_call(
        _ctxattn_kernel,
        grid=(H,),
        in_specs=[spec, spec, spec],
        out_specs=pl.BlockSpec((Lc, hd), lambda h: (0, h)),
        out_shape=jax.ShapeDtypeStruct((Lc, H * hd), BF16),
        compiler_params=_params("arbitrary"),
    )(q, k, v)


def _na_bias(rpb):
    cols = np.arange(GRID_W)
    col_start = np.clip(cols - NA_KW // 2, 0, GRID_W - NA_KW)
    valid = (cols[None, :] >= col_start[:, None]) & (cols[None, :] < col_start[:, None] + NA_KW)
    dc = np.clip(cols[None, :] - cols[:, None] + (NA_KW - 1), 0, 2 * NA_KW - 2)
    tab = rpb[:, :, dc]
    rows = np.arange(NA_KH)[:, None] + np.arange(NA_KH)[None, :]
    slab = tab[:, rows]
    slab = jnp.where(valid[None, None, None], slab, NEG)
    slab = slab.transpose(0, 1, 3, 2, 4)
    return slab.reshape(rpb.shape[0], NA_KH, GRID_W, NA_KH * GRID_W).astype(F32)


def _lru_kernel(*refs, reverse, final, tb, nblk):
    it = iter(refs)
    x_ref, xp_ref, xn_ref = next(it), next(it), next(it)
    cw_ref, cb_ref, wa_ref, ba_ref, wi_ref, bi_ref, lam_ref, h0_ref = (next(it) for _ in range(8))
    hprev_ref, y_ref = (next(it), next(it)) if final else (None, None)
    o_ref, hfin_ref = next(it), next(it)
    xx_scr, a_scr, b_scr, carry_scr = next(it), next(it), next(it), next(it)

    i = pl.program_id(0)
    blk = (nblk - 1 - i) if reverse else i
    S = SUBLANES
    Wd = LRU_WIDTH

    @pl.when(i == 0)
    def _():
        carry_scr[...] = jnp.broadcast_to(h0_ref[...], (S, Wd))

    H = LRU_HALO
    xx_scr[0:H] = jnp.where(blk > 0, xp_ref[...].astype(F32), 0.0)
    xx_scr[H:H + tb] = x_ref[...].astype(F32)
    xx_scr[H + tb:2 * H + tb] = jnp.where(blk < nblk - 1, xn_ref[...].astype(F32), 0.0)
    xc = cb_ref[...]
    for j in range(LRU_CONV):
        xc = xc + cw_ref[j:j + 1] * xx_scr[H - 1 + j:H - 1 + j + tb]

    bw = Wd // LRU_BLOCKS
    ra, ri = [], []
    for n in range(LRU_BLOCKS):
        xb = xc[:, n * bw:(n + 1) * bw].astype(BF16)
        ra.append(_dot(xb, wa_ref[n]))
        ri.append(_dot(xb, wi_ref[n]))
    rg = jax.nn.sigmoid(jnp.concatenate(ra, axis=1) + ba_ref[...])
    ig = jax.nn.sigmoid(jnp.concatenate(ri, axis=1) + bi_ref[...])
    log_a = LRU_C * rg * _log_sigmoid(lam_ref[...])
    a = jnp.exp(log_a)
    a_scr[...] = a
    b_scr[...] = jnp.sqrt(-jnp.tanh(log_a) * (a * a + 1.0)) * (ig * xc)

    row = lax.broadcasted_iota(jnp.int32, (S, Wd), 0)
    ng = tb // S

    def body(gi, carry):
        g = (ng - 1 - gi) if reverse else gi
        off = pl.multiple_of(g * S, S)
        a = a_scr[pl.ds(off, S), :]
        b = b_scr[pl.ds(off, S), :]
        for s in (1, 2, 4):
            sh = (S - s) if reverse else s
            keep = (row < S - s) if reverse else (row >= s)
            b = jnp.where(keep, a * pltpu.roll(b, sh, 0) + b, b)
            a = jnp.where(keep, a * pltpu.roll(a, sh, 0), a)
        h = b + a * carry
        b_scr[pl.ds(off, S), :] = h
        return jnp.broadcast_to(h[0:1] if reverse else h[S - 1:S], (S, Wd))

    carry = lax.fori_loop(0, ng, body, carry_scr[...])
    carry_scr[...] = carry
    hfin_ref[...] = carry
    if final:
        o_ref[...] = ((hprev_ref[...] + b_scr[...]) * jax.nn.gelu(y_ref[...].astype(F32))).astype(o_ref.dtype)
    else:
        o_ref[...] = b_scr[...]


def _lru_pass(z, p, h0, *, reverse, hprev=None):
    T = z.shape[0]
    tb = min(512, T)
    nb = T // tb
    S = SUBLANES
    Wd = LRU_WIDTH
    final = hprev is not None
    blk = (lambda i: nb - 1 - i) if reverse else (lambda i: i)
    H = LRU_HALO
    gpb = tb // H
    xcol = Z_LX // Wd
    row = pl.BlockSpec((1, Wd), lambda i: (0, 0))
    wspec = pl.BlockSpec((LRU_BLOCKS, Wd // LRU_BLOCKS, Wd // LRU_BLOCKS), lambda i: (0, 0, 0))
    args = [z, z, z, p["conv_w"], p["conv_b"], p["w_a"], p["b_a"], p["w_i"], p["b_i"], p["lam"], h0]
    specs = [pl.BlockSpec((tb, Wd), lambda i: (blk(i), xcol)),
             pl.BlockSpec((H, Wd), lambda i: (jnp.maximum(blk(i) * gpb - 1, 0), xcol)),
             pl.BlockSpec((H, Wd), lambda i: (jnp.minimum((blk(i) + 1) * gpb, T // H - 1), xcol)),
             pl.BlockSpec((LRU_CONV, Wd), lambda i: (0, 0)), row, wspec, row, wspec, row, row, row]
    if final:
        args += [hprev, z]
        specs += [pl.BlockSpec((tb, Wd), lambda i: (blk(i), 0)),
                  pl.BlockSpec((tb, Wd), lambda i: (blk(i), Z_LY // Wd))]
    return pl.pallas_call(
        functools.partial(_lru_kernel, reverse=reverse, final=final, tb=tb, nblk=nb),
        grid=(nb,),
        in_specs=specs,
        out_specs=[pl.BlockSpec((tb, Wd), lambda i: (blk(i), 0)), pl.BlockSpec((S, Wd), lambda i: (0, 0))],
        out_shape=[jax.ShapeDtypeStruct((T, Wd), BF16 if final else F32),
                   jax.ShapeDtypeStruct((S, Wd), F32)],
        scratch_shapes=[pltpu.VMEM((tb + 2 * H, Wd), F32), pltpu.VMEM((tb, Wd), F32),
                        pltpu.VMEM((tb, Wd), F32), pltpu.VMEM((S, Wd), F32)],
        compiler_params=_params("arbitrary"),
    )(*args)


def _outproj_kernel(gla_ref, na_ref, lru_ref, x_ref, wg_ref, wn_ref, wl_ref, gate_ref, g2_ref, sh_ref, sc_ref,
                    wr_ref, xo_ref, aff_ref):
    acc = _dot(gla_ref[...], wg_ref[...]) + _dot(na_ref[...], wn_ref[...]) + _dot(lru_ref[...], wl_ref[...])
    xn = x_ref[...] + gate_ref[...] * acc
    xo_ref[...] = xn
    h = _norm2(xn, g2_ref, sh_ref, sc_ref)
    logits = _dot_hi(h, wr_ref[...])
    lane = lax.broadcasted_iota(jnp.int32, logits.shape, 1)
    logits = jnp.where(lane < N_EXPERTS, logits, NEG)
    e = jnp.exp(logits - jnp.max(logits, axis=-1, keepdims=True))
    aff_ref[...] = e / jnp.sum(e, axis=-1, keepdims=True)


def _outproj(gla, na, lru, x, wg, wn, wl, gate, g2, shift, scale, wr):
    T, D = x.shape
    tm = min(512, T)
    rows = lambda w: pl.BlockSpec((tm, w), lambda i: (i, 0))
    full = lambda a: pl.BlockSpec(a.shape, lambda i: (0, 0), pipeline_mode=pl.Buffered(1))
    vec = pl.BlockSpec((1, D), lambda i: (0, 0))
    return pl.pallas_call(
        _outproj_kernel,
        grid=(T // tm,),
        in_specs=[rows(GLA_V), rows(NA_W), rows(LRU_WIDTH), rows(D), full(wg), full(wn), full(wl),
                  vec, vec, vec, vec, full(wr)],
        out_specs=[rows(D), rows(LANES)],
        out_shape=[jax.ShapeDtypeStruct((T, D), F32), jax.ShapeDtypeStruct((T, LANES), F32)],
        compiler_params=_params("arbitrary"),
    )(gla, na, lru, x, wg, wn, wl, gate, g2, shift, scale, wr)


def _gather_kernel(idx_ref, h_hbm, g2_ref, sh_ref, sc_ref, o_ref, buf, sem, *, rt, nt, ne):
    e = pl.program_id(0)
    t = pl.program_id(1)

    step = e * nt + t
    slot = step % 2

    def start_rows(ee, tt, s):
        def issue(r, carry):
            tok = idx_ref[ee, tt * rt + r]
            pltpu.make_async_copy(h_hbm.at[pl.ds(tok, 1)], buf.at[s, pl.ds(r, 1)], sem.at[s]).start()
            return carry

        lax.fori_loop(0, rt, issue, 0, unroll=ISSUE_UNROLL)

    @pl.when(step == 0)
    def _():
        start_rows(0, 0, 0)

    @pl.when(step + 1 < ne * nt)
    def _():
        nxt = step + 1
        start_rows(nxt // nt, nxt % nt, 1 - slot)

    pltpu.make_async_copy(h_hbm.at[pl.ds(0, rt)], buf.at[slot], sem.at[slot]).wait()
    o_ref[0] = _norm2(buf[slot], g2_ref, sh_ref, sc_ref).astype(BF16)


def _gather_rows(x, idx, g2, shift, scale):
    T, D = x.shape
    E, cap = idx.shape
    rt = min(ROW_TILE, cap)
    nt = cap // rt
    vec = pl.BlockSpec((1, D), lambda e, t, idx: (0, 0))
    return pl.pallas_call(
        functools.partial(_gather_kernel, rt=rt, nt=nt, ne=E),
        grid_spec=pltpu.PrefetchScalarGridSpec(
            num_scalar_prefetch=1,
            grid=(E, nt),
            in_specs=[pl.BlockSpec(memory_space=pl.ANY), vec, vec, vec],
            out_specs=pl.BlockSpec((1, rt, D), lambda e, t, idx: (e, t, 0)),
            scratch_shapes=[pltpu.VMEM((2, rt, D), F32), pltpu.SemaphoreType.DMA((2,))]),
        out_shape=jax.ShapeDtypeStruct((E, cap, D), BF16),
        compiler_params=_params("arbitrary", "arbitrary"),
    )(idx, x, g2, shift, scale)


def _combine_kernel(idx_ref, ye_ref, x_in, x_out, buf, gsem, ssem, *, rt, nt):
    del x_in
    e = pl.program_id(0)
    t = pl.program_id(1)
    slot = t % 2

    def start_rows(tile, s, gather):
        def body(r, carry):
            tok = idx_ref[e, tile * rt + r]
            if gather:
                pltpu.make_async_copy(x_out.at[pl.ds(tok, 1)], buf.at[s, pl.ds(r, 1)], gsem.at[s]).start()
            else:
                pltpu.make_async_copy(buf.at[s, pl.ds(r, 1)], x_out.at[pl.ds(tok, 1)], ssem.at[s]).start()
            return carry

        lax.fori_loop(0, rt, body, 0, unroll=ISSUE_UNROLL)

    def wait_rows(s, gather):
        if gather:
            pltpu.make_async_copy(x_out.at[pl.ds(0, rt)], buf.at[s], gsem.at[s]).wait()
        else:
            pltpu.make_async_copy(buf.at[s], x_out.at[pl.ds(0, rt)], ssem.at[s]).wait()

    @pl.when(t == 0)
    def _():
        start_rows(0, 0, True)

    wait_rows(slot, True)

    @pl.when(t + 1 < nt)
    def _():
        @pl.when(t >= 1)
        def _():
            wait_rows(1 - slot, False)

        start_rows(t + 1, 1 - slot, True)

    buf[slot] = buf[slot] + ye_ref[0]
    start_rows(t, slot, False)

    @pl.when(t == nt - 1)
    def _():
        wait_rows(slot, False)
        if nt > 1:
            wait_rows(1 - slot, False)


def _combine(x, ye, idx):
    T, D = x.shape
    E, cap = idx.shape
    rt = min(ROW_TILE, cap)
    nt = cap // rt
    return pl.pallas_call(
        functools.partial(_combine_kernel, rt=rt, nt=nt),
        grid_spec=pltpu.PrefetchScalarGridSpec(
            num_scalar_prefetch=1,
            grid=(E, nt),
            in_specs=[pl.BlockSpec((1, rt, D), lambda e, t, idx: (e, t, 0)),
                      pl.BlockSpec(memory_space=pl.ANY)],
            out_specs=pl.BlockSpec(memory_space=pl.ANY),
            scratch_shapes=[pltpu.VMEM((2, rt, D), F32), pltpu.SemaphoreType.DMA((2,)),
                            pltpu.SemaphoreType.DMA((2,))]),
        out_shape=jax.ShapeDtypeStruct((T, D), F32),
        input_output_aliases={2: 0},
        compiler_params=_params("arbitrary", "arbitrary"),
    )(idx, ye, x)


def _moe_kernel(*refs, nj, has_ctx):
    it = iter(refs)
    xs_ref, tw_ref, gate_ref = next(it), next(it), next(it)
    xc_ref, twc_ref, gatec_ref = (next(it), next(it), next(it)) if has_ctx else (None, None, None)
    wg_ref, wu_ref, wd_ref = next(it), next(it), next(it)
    ye_ref = next(it)
    yc_ref = next(it) if has_ctx else None
    hid_scr = next(it)
    hidc_scr = next(it) if has_ctx else None
    s = pl.program_id(1)
    tf = wg_ref.shape[3]

    def row_chunks(n_rows):
        step = min(MOE_ROW_CHUNK, n_rows)
        return [slice(m * step, (m + 1) * step) for m in range(n_rows // step)]

    def up(x_ref, h_scr):
        wg = wg_ref[0, 0].astype(BF16)
        wu = wu_ref[0, 0].astype(BF16)
        for rs in row_chunks(x_ref.shape[1]):
            x = x_ref[0, rs]
            a = _dot(x, wg)
            h_scr[s, rs] = ((a * jax.nn.sigmoid(a)) * _dot(x, wu)).astype(BF16)

    def down(h_scr, y_ref, w_ref, g_ref):
        wd = wd_ref[0, 0].astype(BF16)
        for rs in row_chunks(y_ref.shape[1]):
            acc = _dot(h_scr[0, rs], wd[0:tf])
            for j in range(1, nj):
                acc = acc + _dot(h_scr[j, rs], wd[j * tf:(j + 1) * tf])
            y_ref[0, rs] = acc * w_ref[0, rs] * g_ref[...]

    @pl.when(s < nj)
    def _():
        up(xs_ref, hid_scr)
        if has_ctx:
            up(xc_ref, hidc_scr)

    @pl.when(s >= nj)
    def _():
        down(hid_scr, ye_ref, tw_ref, gate_ref)
        if has_ctx:
            down(hidc_scr, yc_ref, twc_ref, gatec_ref)


def _moe(xs, tw, gate, layer, w_gate, w_up, w_down, ctx=None):
    E, cap, D = xs.shape
    FF = w_gate.shape[3]
    tf = tn = MOE_TILE
    nj, nn = FF // tf, D // tn
    has_ctx = ctx is not None
    jj = lambda s: jnp.minimum(s, nj - 1)
    nc = lambda s: jnp.maximum(s - nj, 0)
    gvec = pl.BlockSpec((1, tn), lambda e, s: (0, nc(s)))
    args = [xs, tw, gate]
    specs = [pl.BlockSpec((1, cap, D), lambda e, s: (e, 0, 0)),
             pl.BlockSpec((1, cap, 1), lambda e, s: (e, 0, 0)), gvec]
    outs = [pl.BlockSpec((1, cap, tn), lambda e, s: (e, 0, nc(s)))]
    shapes = [jax.ShapeDtypeStruct((E, cap, D), F32)]
    scratch = [pltpu.VMEM((nj, cap, tf), BF16)]
    if has_ctx:
        xc, twc, gatec = ctx
        cc = xc.shape[1]
        args += [xc, twc, gatec]
        specs += [pl.BlockSpec((1, cc, D), lambda e, s: (e, 0, 0)),
                  pl.BlockSpec((1, cc, 1), lambda e, s: (e, 0, 0)), gvec]
        outs.append(pl.BlockSpec((1, cc, tn), lambda e, s: (e, 0, nc(s))))
        shapes.append(jax.ShapeDtypeStruct((E, cc, D), F32))
        scratch.append(pltpu.VMEM((nj, cc, tf), BF16))
    args += [w_gate, w_up, w_down]
    specs += [pl.BlockSpec((1, 1, D, tf), lambda e, s: (layer, e, 0, jj(s))),
              pl.BlockSpec((1, 1, D, tf), lambda e, s: (layer, e, 0, jj(s))),
              pl.BlockSpec((1, 1, FF, tn), lambda e, s: (layer, e, 0, nc(s)))]
    res = pl.pallas_call(
        functools.partial(_moe_kernel, nj=nj, has_ctx=has_ctx),
        grid=(E, nj + nn),
        in_specs=specs,
        out_specs=outs,
        out_shape=shapes,
        scratch_shapes=scratch,
        compiler_params=_params("arbitrary", "arbitrary"),
    )(*args)
    return res if has_ctx else (res[0], None)


def _route(aff, x, g2, shift, scale):
    cap = CAPACITY * x.shape[0] // N_EXPERTS
    top_w, top_idx = lax.top_k(aff[:, :N_EXPERTS].T, cap)
    return _gather_rows(x, top_idx, g2, shift, scale), top_w[..., None], top_idx


def _reorder_w_in(w):
    D = w.shape[0]
    sizes = (GLA_QK, GLA_QK, GLA_V, GLA_V, GLA_GATE_RANK, GLA_GATE_RANK, NA_W, NA_W, NA_W, LRU_WIDTH, LRU_WIDTH)
    parts, o = [], 0
    for s in sizes:
        parts.append(w[:, o:o + s])
        o += s
    gq, gk, gv, gr, lf, lb, nq, nk, nv, lx, ly = parts
    pad = jnp.zeros((D, LANES - 2 * GLA_GATE_RANK), w.dtype)
    main = jnp.concatenate([nq, nk, nv, gv, gr, lx, ly, gq, gk], axis=1).astype(BF16)
    return main, jnp.concatenate([lf, lb, pad], axis=1).astype(BF16)


def kernel(x, c, ctx, c_ctx, norm1_g, norm2_g, w_mod, b_mod, w_in, gla_w_gate, gla_b_gate, gla_norm_g, na_q_norm_g, na_k_norm_g, na_rpb, lru_conv_w, lru_conv_b, lru_w_a, lru_b_a, lru_w_i, lru_b_i, lru_lambda, w_out, w_router, w_exp_gate, w_exp_up, w_exp_down):
    B, T, D = x.shape
    Lc = ctx.shape[1]
    depth = w_in.shape[0]
    assert B == 1 and T % (NA_ROWS_PER_STEP * GRID_W) == 0 and T // GRID_W >= NA_KH
    xs = x[0]
    cs = ctx[0]
    mod = _modulation(jnp.stack([c[0], c_ctx]), w_mod, b_mod)
    cos, sin = _rope_tables(T)
    tb_l, tb_c = min(512, T), min(512, Lc)
    tri = {(tb, rev): _tri_matrix(tb, rev) for tb in {tb_l, tb_c} for rev in (False, True)}
    row = lambda v: v.reshape(1, -1)

    for l in range(depth):
        ctx_out = l < depth - 1
        mx = [row(m) for m in jnp.split(mod[l, 0], 6)]
        mc = [row(m) for m in jnp.split(mod[l, 1], 6)]
        w_main, w_lr = _reorder_w_in(w_in[l])
        g1 = row(norm1_g[l])
        zx, zx_lr = _inproj(xs, g1, mx[0], mx[1], w_main, w_lr)
        zc, zc_lr = _inproj(cs, g1, mc[0], mc[1], w_main, w_lr)

        gla_dir = []
        for d in range(2):
            wg = jnp.zeros((LANES, GLA_QK), F32).at[d * GLA_GATE_RANK:(d + 1) * GLA_GATE_RANK].set(gla_w_gate[l, d])
            gla_dir.append((wg, row(gla_b_gate[l, d])))
        ng = row(gla_norm_g[l])
        zero_state = jnp.zeros((GLA_HEADS, GLA_DV, GLA_DK), F32)
        oc_f, sc_f = _gla_pass(zc, zc_lr, None, None, *gla_dir[0], tri[(tb_c, False)], zero_state, reverse=False)
        gla_c, sc_b = _gla_pass(zc, zc_lr, None, None, *gla_dir[1], tri[(tb_c, True)], zero_state, reverse=True,
                                oprev=oc_f, norm_g=ng)
        ol_f, _ = _gla_pass(zx, zx_lr, cos, sin, *gla_dir[0], tri[(tb_l, False)], sc_f, reverse=False)
        gla_x, _ = _gla_pass(zx, zx_lr, cos, sin, *gla_dir[1], tri[(tb_l, True)], sc_b, reverse=True,
                             oprev=ol_f, norm_g=ng)

        qg, kg = row(na_q_norm_g[l]), row(na_k_norm_g[l])
        ql, kl, vl = _na_prep(zx, qg, kg)
        qc, kc, vc = _na_prep(zc, qg, kg)
        na_x = _na_latent(ql, kl, vl, kc, vc, _na_bias(na_rpb[l]))

        lru_dir = []
        for d in range(2):
            lru_dir.append(dict(conv_w=lru_conv_w[l], conv_b=row(lru_conv_b[l]),
                                w_a=lru_w_a[l, d].astype(BF16), b_a=row(lru_b_a[l, d]),
                                w_i=lru_w_i[l, d].astype(BF16), b_i=row(lru_b_i[l, d]),
                                lam=row(lru_lambda[l, d])))
        zero_h = jnp.zeros((1, LRU_WIDTH), F32)
        hc_f, fin_f = _lru_pass(zc, lru_dir[0], zero_h, reverse=False)
        lru_c, fin_b = _lru_pass(zc, lru_dir[1], zero_h, reverse=True, hprev=hc_f)
        hl_f, _ = _lru_pass(zx, lru_dir[0], fin_f[0:1], reverse=False)
        lru_x, _ = _lru_pass(zx, lru_dir[1], fin_b[0:1], reverse=True, hprev=hl_f)

        wo = w_out[l].astype(BF16)
        wo_g, wo_n, wo_l = wo[:GLA_V], wo[GLA_V:GLA_V + NA_W], wo[GLA_V + NA_W:]
        wr = jnp.zeros((D, LANES), F32).at[:, :N_EXPERTS].set(w_router[l])
        g2 = row(norm2_g[l])
        x1, affx = _outproj(gla_x, na_x, lru_x, xs, wo_g, wo_n, wo_l, mx[2], g2, mx[3], mx[4], wr)
        xg, twx, idxx = _route(affx, x1, g2, mx[3], mx[4])
        if ctx_out:
            na_c = _na_context(qc, kc, vc)
            c1, affc = _outproj(gla_c, na_c, lru_c, cs, wo_g, wo_n, wo_l, mc[2], g2, mc[3], mc[4], wr)
            cg, twc, idxc = _route(affc, c1, g2, mc[3], mc[4])
            ye, yc = _moe(xg, twx, mx[5], l, w_exp_gate, w_exp_up, w_exp_down, ctx=(cg, twc, mc[5]))
            cs = _combine(c1, yc, idxc)
        else:
            ye, _ = _moe(xg, twx, mx[5], l, w_exp_gate, w_exp_up, w_exp_down)
        xs = _combine(x1, ye, idxx)
    return xs[None]
```

```python
import functools

import numpy as np
import jax
import jax.numpy as jnp
from jax import lax
from jax.experimental import pallas as pl
from jax.experimental.pallas import tpu as pltpu

F32 = jnp.float32
BF16 = jnp.bfloat16

GRID_W = 64
EPS = 1e-6
GLA_HEADS = 4
GLA_DK = 64
GLA_DV = 128
GLA_GATE_RANK = 16
GLA_TAU = 16.0
GLA_CHUNK = 64
ROPE_BASE = 10000.0
NA_HEADS = 8
NA_HD = 128
NA_KH = 8
NA_KW = 16
LRU_WIDTH = 512
LRU_BLOCKS = 4
LRU_CONV = 4
LRU_C = 8.0
N_EXPERTS = 16
CAPACITY = 2

GLA_QK = GLA_HEADS * GLA_DK
GLA_V = GLA_HEADS * GLA_DV
NA_W = NA_HEADS * NA_HD

LANES = 128
SUBLANES = 8
VMEM_LIMIT = 56 * 1024 * 1024
NEG = -1e30
ROW_TILE = 256
ISSUE_UNROLL = 8
MOE_TILE = 256
MOE_ROW_CHUNK = 512
LRU_HALO = 16
NA_ROWS_PER_STEP = 8
NA_UNION_ROWS = 16
NA_Q_CHUNK = 256

Z_NAQ, Z_NAK, Z_NAV = 0, NA_W, 2 * NA_W
Z_GV = 3 * NA_W
Z_GR = Z_GV + GLA_V
Z_LX = Z_GR + GLA_V
Z_LY = Z_LX + LRU_WIDTH
Z_GQ = Z_LY + LRU_WIDTH
Z_GK = Z_GQ + GLA_QK
Z_COLS = Z_GK + GLA_QK
Z_CHUNK = 512


def _params(*sem):
    return pltpu.CompilerParams(dimension_semantics=sem, vmem_limit_bytes=VMEM_LIMIT)


def _dot(a, b):
    return jnp.dot(a, b, preferred_element_type=F32)


def _dot_nt(a, b):
    return lax.dot_general(a, b, (((1,), (1,)), ((), ())), preferred_element_type=F32)


def _dot_tn(a, b):
    return lax.dot_general(a, b, (((0,), (0,)), ((), ())), preferred_element_type=F32)


def _split2(a):
    hi = a.astype(BF16)
    lo = (a - hi.astype(F32)).astype(BF16)
    return hi, lo


def _dot_hi(a, b):
    ah, al = _split2(a)
    bh, bl = _split2(b)
    return _dot(ah, bh) + _dot(al, bh) + _dot(ah, bl)


def _rms(x):
    return x * lax.rsqrt(jnp.mean(x * x, axis=-1, keepdims=True) + EPS)


def _norm2(x, g_ref, sh_ref, sc_ref):
    return (_rms(x) * g_ref[...]) * (1.0 + sc_ref[...]) + sh_ref[...]


def _log_sigmoid(z):
    return jnp.minimum(z, 0.0) - jnp.log1p(jnp.exp(-jnp.abs(z)))


def _mod_kernel(cb_ref, w_ref, b_ref, o_ref):
    tn = w_ref.shape[2]
    s = [cb_ref[v] * jax.nn.sigmoid(cb_ref[v]) for v in range(2)]
    for j in range(tn // LANES):
        cs = slice(j * LANES, (j + 1) * LANES)
        wj = w_ref[0, :, cs]
        for v in range(2):
            o_ref[0, v:v + 1, cs] = jnp.sum(wj * s[v], axis=0, keepdims=True) + b_ref[0, :, cs]


def _modulation(c2, w_mod, b_mod):
    L, D, N6 = w_mod.shape
    tn = 1024
    cb = jnp.broadcast_to(c2[:, :, None], (2, D, LANES))
    return pl.pallas_call(
        _mod_kernel,
        grid=(L, N6 // tn),
        in_specs=[pl.BlockSpec((2, D, LANES), lambda l, j: (0, 0, 0)),
                  pl.BlockSpec((1, D, tn), lambda l, j: (l, 0, j)),
                  pl.BlockSpec((1, 1, tn), lambda l, j: (l, 0, j))],
        out_specs=pl.BlockSpec((1, 2, tn), lambda l, j: (l, 0, j)),
        out_shape=jax.ShapeDtypeStruct((L, 2, N6), F32),
        compiler_params=_params("arbitrary", "arbitrary"),
    )(cb, w_mod, b_mod.reshape(L, 1, N6))


def _inproj_kernel(x_ref, g_ref, sh_ref, sc_ref, w_ref, wlr_ref, o_ref, olr_ref):
    h = _norm2(x_ref[...], g_ref, sh_ref, sc_ref).astype(BF16)
    for n in range(w_ref.shape[1] // Z_CHUNK):
        cs = slice(n * Z_CHUNK, (n + 1) * Z_CHUNK)
        o_ref[:, cs] = _dot(h, w_ref[:, cs]).astype(BF16)
    olr_ref[...] = _dot(h, wlr_ref[...])


def _inproj(x, gain, shift, scale, w, wlr):
    T, D = x.shape
    NZ = w.shape[1]
    tm = min(512, T)
    vec = pl.BlockSpec((1, D), lambda i: (0, 0))
    resident = lambda a: pl.BlockSpec(a.shape, lambda i: (0, 0), pipeline_mode=pl.Buffered(1))
    return pl.pallas_call(
        _inproj_kernel,
        grid=(T // tm,),
        in_specs=[pl.BlockSpec((tm, D), lambda i: (i, 0)), vec, vec, vec, resident(w), resident(wlr)],
        out_specs=[pl.BlockSpec((tm, NZ), lambda i: (i, 0)), pl.BlockSpec((tm, LANES), lambda i: (i, 0))],
        out_shape=[jax.ShapeDtypeStruct((T, NZ), BF16), jax.ShapeDtypeStruct((T, LANES), F32)],
        compiler_params=_params("arbitrary"),
    )(x, gain, shift, scale, w, wlr)


def _gla_kernel(*refs, reverse, rope, final, tb):
    it = iter(refs)
    q_ref, k_ref, v_ref, lr_ref = next(it), next(it), next(it), next(it)
    cos_ref, sin_ref = (next(it), next(it)) if rope else (None, None)
    wg_ref, bg_ref, tri_ref, s0_ref = next(it), next(it), next(it), next(it)
    oprev_ref, r_ref, ng_ref = (next(it), next(it), next(it)) if final else (None, None, None)
    o_ref, sfin_ref, st_scr = next(it), next(it), next(it)

    @pl.when(pl.program_id(0) == 0)
    def _():
        st_scr[...] = s0_ref[...]

    q = q_ref[...].astype(F32)
    k = k_ref[...].astype(F32)
    if rope:
        lane = lax.broadcasted_iota(jnp.int32, q.shape, 1)
        first = (lane % GLA_DK) < (GLA_DK // 2)
        cos, sin = cos_ref[...], sin_ref[...]

        def rot(t):
            return jnp.where(first, pltpu.roll(t, GLA_QK - GLA_DK // 2, 1), pltpu.roll(t, GLA_DK // 2, 1))

        q = q * cos + rot(q) * sin
        k = k * cos + rot(k) * sin
    q = q * (GLA_DK ** -0.5)

    g = _log_sigmoid(_dot_hi(lr_ref[...], wg_ref[...]) + bg_ref[...]) * (1.0 / GLA_TAU)
    g_hi = g.astype(BF16)
    g_r1 = g - g_hi.astype(F32)
    g_mid = g_r1.astype(BF16)
    g_lo = (g_r1 - g_mid.astype(F32)).astype(BF16)
    tri = tri_ref[...]
    gc = _dot(tri, g_hi) + _dot(tri, g_mid) + _dot(tri, g_lo)

    C = GLA_CHUNK
    ri = lax.broadcasted_iota(jnp.int32, (C, C), 0)
    ci = lax.broadcasted_iota(jnp.int32, (C, C), 1)
    causal = (ci >= ri) if reverse else (ci <= ri)
    nchunk = tb // C
    for c in (range(nchunk - 1, -1, -1) if reverse else range(nchunk)):
        sl = slice(c * C, (c + 1) * C)
        gcc = gc[sl]
        gt = gcc[0:1] if reverse else gcc[C - 1:C]
        qe = q[sl] * jnp.exp(gcc)
        ke = k[sl] * jnp.exp(-gcc)
        kd = k[sl] * jnp.exp(gt - gcc)
        dec = jnp.exp(gt)
        for h in range(GLA_HEADS):
            hs = slice(h * GLA_DK, (h + 1) * GLA_DK)
            vs = slice(h * GLA_DV, (h + 1) * GLA_DV)
            qh = qe[:, hs].astype(BF16)
            vh = v_ref[sl, vs]
            att = jnp.where(causal, _dot_nt(qh, ke[:, hs].astype(BF16)), 0.0)
            st = st_scr[h]
            o = _dot(att.astype(BF16), vh) + _dot_nt(qh, st.astype(BF16))
            st_scr[h] = st * dec[:, hs] + _dot_tn(vh, kd[:, hs].astype(BF16))
            if final:
                y = _rms(oprev_ref[sl, vs] + o) * ng_ref[...]
                rg = r_ref[sl, vs].astype(F32)
                o_ref[sl, vs] = (y * (rg * jax.nn.sigmoid(rg))).astype(o_ref.dtype)
            else:
                o_ref[sl, vs] = o
    sfin_ref[...] = st_scr[...]


def _gla_pass(z, zlr, cos, sin, wg, bg, tri, s0, *, reverse, oprev=None, norm_g=None):
    T = z.shape[0]
    tb = tri.shape[0]
    nb = T // tb
    rope = cos is not None
    final = oprev is not None
    blk = (lambda i: nb - 1 - i) if reverse else (lambda i: i)

    def col(width, start):
        return pl.BlockSpec((tb, width), lambda i: (blk(i), start // width))

    const2 = lambda shape: pl.BlockSpec(shape, lambda i: (0, 0))
    state = pl.BlockSpec((GLA_HEADS, GLA_DV, GLA_DK), lambda i: (0, 0, 0))
    args = [z, z, z, zlr]
    specs = [col(GLA_QK, Z_GQ), col(GLA_QK, Z_GK), col(GLA_V, Z_GV), col(LANES, 0)]
    if rope:
        args += [cos, sin]
        specs += [col(GLA_QK, 0), col(GLA_QK, 0)]
    args += [wg, bg, tri, s0]
    specs += [const2((LANES, GLA_QK)), const2((1, GLA_QK)), const2((tb, tb)), state]
    if final:
        args += [oprev, z, norm_g]
        specs += [col(GLA_V, 0), col(GLA_V, Z_GR), const2((1, GLA_DV))]
    return pl.pallas_call(
        functools.partial(_gla_kernel, reverse=reverse, rope=rope, final=final, tb=tb),
        grid=(nb,),
        in_specs=specs,
        out_specs=[col(GLA_V, 0), state],
        out_shape=[jax.ShapeDtypeStruct((T, GLA_V), BF16 if final else F32),
                   jax.ShapeDtypeStruct((GLA_HEADS, GLA_DV, GLA_DK), F32)],
        scratch_shapes=[pltpu.VMEM((GLA_HEADS, GLA_DV, GLA_DK), F32)],
        compiler_params=_params("arbitrary"),
    )(*args)


def _tri_matrix(tb, reverse):
    i = np.arange(tb)
    same = (i[:, None] // GLA_CHUNK) == (i[None, :] // GLA_CHUNK)
    order = (i[None, :] >= i[:, None]) if reverse else (i[None, :] <= i[:, None])
    return jnp.asarray(same & order, BF16)


def _rope_tables(T):
    pos = jnp.arange(T)
    row = (pos // GRID_W).astype(F32)
    col = (pos % GRID_W).astype(F32)
    nf = GLA_DK // 4
    inv = ROPE_BASE ** (-jnp.arange(nf, dtype=F32) / nf)
    ang = jnp.concatenate([row[:, None] * inv, col[:, None] * inv], axis=-1)
    cos, sin = jnp.cos(ang), jnp.sin(ang)
    cos_full = jnp.tile(jnp.concatenate([cos, cos], axis=-1), (1, GLA_HEADS))
    sin_full = jnp.tile(jnp.concatenate([-sin, sin], axis=-1), (1, GLA_HEADS))
    return cos_full, sin_full


def _naprep_kernel(q_ref, k_ref, v_ref, qg_ref, kg_ref, qo_ref, ko_ref, vo_ref):
    for h in range(NA_HEADS):
        hs = slice(h * NA_HD, (h + 1) * NA_HD)
        qo_ref[h] = (_rms(q_ref[:, hs].astype(F32)) * qg_ref[...] * (NA_HD ** -0.5)).astype(BF16)
        ko_ref[h] = (_rms(k_ref[:, hs].astype(F32)) * kg_ref[...]).astype(BF16)
        vo_ref[h] = v_ref[:, hs]


def _na_prep(z, qg, kg):
    T = z.shape[0]
    tm = min(512, T)
    col = lambda c: pl.BlockSpec((tm, NA_W), lambda i: (i, c))
    vec = pl.BlockSpec((1, NA_HD), lambda i: (0, 0))
    out = pl.BlockSpec((NA_HEADS, tm, NA_HD), lambda i: (0, i, 0))
    shp = jax.ShapeDtypeStruct((NA_HEADS, T, NA_HD), BF16)
    return pl.pallas_call(
        _naprep_kernel,
        grid=(T // tm,),
        in_specs=[col(Z_NAQ // NA_W), col(Z_NAK // NA_W), col(Z_NAV // NA_W), vec, vec],
        out_specs=[out, out, out],
        out_shape=[shp, shp, shp],
        compiler_params=_params("arbitrary"),
    )(z, z, z, qg, kg)


def _na_kernel(q_ref, k_ref, v_ref, kc_ref, vc_ref, bias_ref, o_ref, *, n_rows):
    i = pl.program_id(1)
    W = GRID_W
    first_row = jnp.clip(i * NA_ROWS_PER_STEP - NA_KH // 2, 0, n_rows - NA_UNION_ROWS)
    start = pl.multiple_of(first_row * W, W)
    kw = k_ref[0, pl.ds(start, NA_UNION_ROWS * W), :]
    vw = v_ref[0, pl.ds(start, NA_UNION_ROWS * W), :]
    kc = kc_ref[0]
    vc = vc_ref[0]
    nq = q_ref.shape[1]
    step = min(NA_Q_CHUNK, nq)
    for c in range(nq // step):
        qs = slice(c * step, (c + 1) * step)
        q = q_ref[0, qs, :]
        s_loc = _dot_nt(q, kw) + bias_ref[0, 0, qs, :]
        s_ctx = _dot_nt(q, kc)
        m = jnp.maximum(jnp.max(s_loc, axis=-1, keepdims=True), jnp.max(s_ctx, axis=-1, keepdims=True))
        p_loc = jnp.exp(s_loc - m)
        p_ctx = jnp.exp(s_ctx - m)
        den = jnp.sum(p_loc, axis=-1, keepdims=True) + jnp.sum(p_ctx, axis=-1, keepdims=True)
        o = _dot(p_loc.astype(BF16), vw) + _dot(p_ctx.astype(BF16), vc)
        o_ref[qs, :] = (o / den).astype(BF16)


def _na_latent(q, k, v, kc, vc, bias):
    H, T, hd = q.shape
    Lc = kc.shape[1]
    n_rows = T // GRID_W
    R = NA_ROWS_PER_STEP
    nblk = n_rows // R
    whole = lambda n: pl.BlockSpec((1, n, hd), lambda h, i: (h, 0, 0))
    variant = lambda i: jnp.where(i == 0, 0, jnp.where(i == nblk - 1, 2, 1))
    return pl.pallas_call(
        functools.partial(_na_kernel, n_rows=n_rows),
        grid=(H, nblk),
        in_specs=[pl.BlockSpec((1, R * GRID_W, hd), lambda h, i: (h, i, 0)),
                  whole(T), whole(T), whole(Lc), whole(Lc),
                  pl.BlockSpec((1, 1, R * GRID_W, NA_UNION_ROWS * GRID_W), lambda h, i: (h, variant(i), 0, 0))],
        out_specs=pl.BlockSpec((R * GRID_W, hd), lambda h, i: (i, h)),
        out_shape=jax.ShapeDtypeStruct((T, H * hd), BF16),
        compiler_params=_params("arbitrary", "arbitrary"),
    )(q, k, v, kc, vc, bias)


def _ctxattn_kernel(q_ref, k_ref, v_ref, o_ref):
    s = _dot_nt(q_ref[0], k_ref[0])
    p = jnp.exp(s - jnp.max(s, axis=-1, keepdims=True))
    o = _dot(p.astype(BF16), v_ref[0]) / jnp.sum(p, axis=-1, keepdims=True)
    o_ref[...] = o.astype(BF16)


def _na_context(q, k, v):
    H, Lc, hd = q.shape
    spec = pl.BlockSpec((1, Lc, hd), lambda h: (h, 0, 0))
    return pl.pallas_call(
        _ctxattn_kernel,
        grid=(H,),
        in_specs=[spec, spec, spec],
        out_specs=pl.BlockSpec((Lc, hd), lambda h: (0, h)),
        out_shape=jax.ShapeDtypeStruct((Lc, H * hd), BF16),
        compiler_params=_params("arbitrary"),
    )(q, k, v)


def _na_bias(rpb, n_rows):
    R, U, W = NA_ROWS_PER_STEP, NA_UNION_ROWS, GRID_W
    cols = np.arange(W)
    col_start = np.clip(cols - NA_KW // 2, 0, W - NA_KW)
    col_ok = (cols[None, :] >= col_start[:, None]) & (cols[None, :] < col_start[:, None] + NA_KW)
    dc = np.clip(cols[None, :] - cols[:, None] + (NA_KW - 1), 0, 2 * NA_KW - 2)
    tab = jnp.where(col_ok[None, None], rpb[:, :, dc], NEG)
    didx = np.zeros((3, R, U), np.int32)
    row_ok = np.zeros((3, R, U), bool)
    for v, r0 in enumerate((0, R, n_rows - R)):
        first_row = np.clip(r0 - NA_KH // 2, 0, n_rows - U)
        for rr in range(R):
            r = r0 + rr
            rs = np.clip(r - NA_KH // 2, 0, n_rows - NA_KH)
            kr = first_row + np.arange(U)
            row_ok[v, rr] = (kr >= rs) & (kr < rs + NA_KH)
            didx[v, rr] = np.clip(kr - r + (NA_KH - 1), 0, 2 * NA_KH - 2)
    slab = jnp.where(row_ok[None, :, :, :, None, None], tab[:, didx], NEG)
    slab = slab.transpose(0, 1, 2, 4, 3, 5)
    return slab.reshape(rpb.shape[0], 3, R * W, U * W).astype(F32)


def _lru_kernel(*refs, reverse, final, tb, nblk):
    it = iter(refs)
    x_ref, xp_ref, xn_ref = next(it), next(it), next(it)
    cw_ref, cb_ref, wa_ref, ba_ref, wi_ref, bi_ref, lam_ref, h0_ref = (next(it) for _ in range(8))
    hprev_ref, y_ref = (next(it), next(it)) if final else (None, None)
    o_ref, hfin_ref = next(it), next(it)
    xx_scr, a_scr, b_scr, carry_scr = next(it), next(it), next(it), next(it)

    i = pl.program_id(0)
    blk = (nblk - 1 - i) if reverse else i
    S = SUBLANES
    Wd = LRU_WIDTH

    @pl.when(i == 0)
    def _():
        carry_scr[...] = jnp.broadcast_to(h0_ref[...], (S, Wd))

    H = LRU_HALO
    xx_scr[0:H] = jnp.where(blk > 0, xp_ref[...].astype(F32), 0.0)
    xx_scr[H:H + tb] = x_ref[...].astype(F32)
    xx_scr[H + tb:2 * H + tb] = jnp.where(blk < nblk - 1, xn_ref[...].astype(F32), 0.0)
    xc = cb_ref[...]
    for j in range(LRU_CONV):
        xc = xc + cw_ref[j:j + 1] * xx_scr[H - 1 + j:H - 1 + j + tb]

    bw = Wd // LRU_BLOCKS
    ra, ri = [], []
    for n in range(LRU_BLOCKS):
        xb = xc[:, n * bw:(n + 1) * bw].astype(BF16)
        ra.append(_dot(xb, wa_ref[n]))
        ri.append(_dot(xb, wi_ref[n]))
    rg = jax.nn.sigmoid(jnp.concatenate(ra, axis=1) + ba_ref[...])
    ig = jax.nn.sigmoid(jnp.concatenate(ri, axis=1) + bi_ref[...])
    log_a = LRU_C * rg * _log_sigmoid(lam_ref[...])
    a = jnp.exp(log_a)
    a_scr[...] = a
    b_scr[...] = jnp.sqrt(-jnp.tanh(log_a) * (a * a + 1.0)) * (ig * xc)

    row = lax.broadcasted_iota(jnp.int32, (S, Wd), 0)
    ng = tb // S

    def body(gi, carry):
        g = (ng - 1 - gi) if reverse else gi
        off = pl.multiple_of(g * S, S)
        a = a_scr[pl.ds(off, S), :]
        b = b_scr[pl.ds(off, S), :]
        for s in (1, 2, 4):
            sh = (S - s) if reverse else s
            keep = (row < S - s) if reverse else (row >= s)
            b = jnp.where(keep, a * pltpu.roll(b, sh, 0) + b, b)
            a = jnp.where(keep, a * pltpu.roll(a, sh, 0), a)
        h = b + a * carry
        b_scr[pl.ds(off, S), :] = h
        return jnp.broadcast_to(h[0:1] if reverse else h[S - 1:S], (S, Wd))

    carry = lax.fori_loop(0, ng, body, carry_scr[...])
    carry_scr[...] = carry
    hfin_ref[...] = carry
    if final:
        o_ref[...] = ((hprev_ref[...] + b_scr[...]) * jax.nn.gelu(y_ref[...].astype(F32))).astype(o_ref.dtype)
    else:
        o_ref[...] = b_scr[...]


def _lru_pass(z, p, h0, *, reverse, hprev=None):
    T = z.shape[0]
    tb = min(512, T)
    nb = T // tb
    S = SUBLANES
    Wd = LRU_WIDTH
    final = hprev is not None
    blk = (lambda i: nb - 1 - i) if reverse else (lambda i: i)
    H = LRU_HALO
    gpb = tb // H
    xcol = Z_LX // Wd
    row = pl.BlockSpec((1, Wd), lambda i: (0, 0))
    wspec = pl.BlockSpec((LRU_BLOCKS, Wd // LRU_BLOCKS, Wd // LRU_BLOCKS), lambda i: (0, 0, 0))
    args = [z, z, z, p["conv_w"], p["conv_b"], p["w_a"], p["b_a"], p["w_i"], p["b_i"], p["lam"], h0]
    specs = [pl.BlockSpec((tb, Wd), lambda i: (blk(i), xcol)),
             pl.BlockSpec((H, Wd), lambda i: (jnp.maximum(blk(i) * gpb - 1, 0), xcol)),
             pl.BlockSpec((H, Wd), lambda i: (jnp.minimum((blk(i) + 1) * gpb, T // H - 1), xcol)),
             pl.BlockSpec((LRU_CONV, Wd), lambda i: (0, 0)), row, wspec, row, wspec, row, row, row]
    if final:
        args += [hprev, z]
        specs += [pl.BlockSpec((tb, Wd), lambda i: (blk(i), 0)),
                  pl.BlockSpec((tb, Wd), lambda i: (blk(i), Z_LY // Wd))]
    return pl.pallas_call(
        functools.partial(_lru_kernel, reverse=reverse, final=final, tb=tb, nblk=nb),
        grid=(nb,),
        in_specs=specs,
        out_specs=[pl.BlockSpec((tb, Wd), lambda i: (blk(i), 0)), pl.BlockSpec((S, Wd), lambda i: (0, 0))],
        out_shape=[jax.ShapeDtypeStruct((T, Wd), BF16 if final else F32),
                   jax.ShapeDtypeStruct((S, Wd), F32)],
        scratch_shapes=[pltpu.VMEM((tb + 2 * H, Wd), F32), pltpu.VMEM((tb, Wd), F32),
                        pltpu.VMEM((tb, Wd), F32), pltpu.VMEM((S, Wd), F32)],
        compiler_params=_params("arbitrary"),
    )(*args)


def _outproj_kernel(gla_ref, na_ref, lru_ref, x_ref, wo_ref, gate_ref, g2_ref, sh_ref, sc_ref,
                    wr_ref, xo_ref, aff_ref, mix_scr):
    mix_scr[:, 0:GLA_V] = gla_ref[...]
    mix_scr[:, GLA_V:GLA_V + NA_W] = na_ref[...]
    mix_scr[:, GLA_V + NA_W:] = lru_ref[...]
    xn = x_ref[...] + gate_ref[...] * _dot(mix_scr[...], wo_ref[...])
    xo_ref[...] = xn
    h = _norm2(xn, g2_ref, sh_ref, sc_ref)
    logits = _dot(h.astype(BF16), wr_ref[...])
    lane = lax.broadcasted_iota(jnp.int32, logits.shape, 1)
    logits = jnp.where(lane < N_EXPERTS, logits, NEG)
    e = jnp.exp(logits - jnp.max(logits, axis=-1, keepdims=True))
    aff_ref[...] = e / jnp.sum(e, axis=-1, keepdims=True)


def _outproj(gla, na, lru, x, wo, gate, g2, shift, scale, wr):
    T, D = x.shape
    tm = min(512, T)
    rows = lambda w: pl.BlockSpec((tm, w), lambda i: (i, 0))
    full = lambda a: pl.BlockSpec(a.shape, lambda i: (0, 0), pipeline_mode=pl.Buffered(1))
    vec = pl.BlockSpec((1, D), lambda i: (0, 0))
    return pl.pallas_call(
        _outproj_kernel,
        grid=(T // tm,),
        in_specs=[rows(GLA_V), rows(NA_W), rows(LRU_WIDTH), rows(D), full(wo), vec, vec, vec, vec, full(wr)],
        out_specs=[rows(D), rows(LANES)],
        out_shape=[jax.ShapeDtypeStruct((T, D), F32), jax.ShapeDtypeStruct((T, LANES), F32)],
        scratch_shapes=[pltpu.VMEM((tm, wo.shape[0]), BF16)],
        compiler_params=_params("arbitrary"),
    )(gla, na, lru, x, wo, gate, g2, shift, scale, wr)


def _gather_kernel(idx_ref, h_hbm, g2_ref, sh_ref, sc_ref, o_ref, buf, sem, *, rt, nt, ne):
    e = pl.program_id(0)
    t = pl.program_id(1)

    step = e * nt + t
    slot = step % 2

    def start_rows(ee, tt, s):
        def issue(r, carry):
            tok = idx_ref[ee, tt * rt + r]
            pltpu.make_async_copy(h_hbm.at[pl.ds(tok, 1)], buf.at[s, pl.ds(r, 1)], sem.at[s]).start()
            return carry

        lax.fori_loop(0, rt, issue, 0, unroll=ISSUE_UNROLL)

    @pl.when(step == 0)
    def _():
        start_rows(0, 0, 0)

    @pl.when(step + 1 < ne * nt)
    def _():
        nxt = step + 1
        start_rows(nxt // nt, nxt % nt, 1 - slot)

    pltpu.make_async_copy(h_hbm.at[pl.ds(0, rt)], buf.at[slot], sem.at[slot]).wait()
    o_ref[0] = _norm2(buf[slot], g2_ref, sh_ref, sc_ref).astype(BF16)


def _gather_rows(x, idx, g2, shift, scale):
    T, D = x.shape
    E, cap = idx.shape
    rt = min(ROW_TILE, cap)
    nt = cap // rt
    vec = pl.BlockSpec((1, D), lambda e, t, idx: (0, 0))
    return pl.pallas_call(
        functools.partial(_gather_kernel, rt=rt, nt=nt, ne=E),
        grid_spec=pltpu.PrefetchScalarGridSpec(
            num_scalar_prefetch=1,
            grid=(E, nt),
            in_specs=[pl.BlockSpec(memory_space=pl.ANY), vec, vec, vec],
            out_specs=pl.BlockSpec((1, rt, D), lambda e, t, idx: (e, t, 0)),
            scratch_shapes=[pltpu.VMEM((2, rt, D), F32), pltpu.SemaphoreType.DMA((2,))]),
        out_shape=jax.ShapeDtypeStruct((E, cap, D), BF16),
        compiler_params=_params("arbitrary", "arbitrary"),
    )(idx, x, g2, shift, scale)


def _combine_kernel(idx_ref, ye_ref, x_in, x_out, buf, gsem, ssem, *, rt, nt):
    del x_in
    e = pl.program_id(0)
    t = pl.program_id(1)
    slot = t % 2

    def start_rows(tile, s, gather):
        def body(r, carry):
            tok = idx_ref[e, tile * rt + r]
            if gather:
                pltpu.make_async_copy(x_out.at[pl.ds(tok, 1)], buf.at[s, pl.ds(r, 1)], gsem.at[s]).start()
            else:
                pltpu.make_async_copy(buf.at[s, pl.ds(r, 1)], x_out.at[pl.ds(tok, 1)], ssem.at[s]).start()
            return carry

        lax.fori_loop(0, rt, body, 0, unroll=ISSUE_UNROLL)

    def wait_rows(s, gather):
        if gather:
            pltpu.make_async_copy(x_out.at[pl.ds(0, rt)], buf.at[s], gsem.at[s]).wait()
        else:
            pltpu.make_async_copy(buf.at[s], x_out.at[pl.ds(0, rt)], ssem.at[s]).wait()

    @pl.when(t == 0)
    def _():
        start_rows(0, 0, True)

    wait_rows(slot, True)

    @pl.when(t + 1 < nt)
    def _():
        @pl.when(t >= 1)
        def _():
            wait_rows(1 - slot, False)

        start_rows(t + 1, 1 - slot, True)

    buf[slot] = buf[slot] + ye_ref[0]
    start_rows(t, slot, False)

    @pl.when(t == nt - 1)
    def _():
        wait_rows(slot, False)
        if nt > 1:
            wait_rows(1 - slot, False)


def _combine(x, ye, idx):
    T, D = x.shape
    E, cap = idx.shape
    rt = min(ROW_TILE, cap)
    nt = cap // rt
    return pl.pallas_call(
        functools.partial(_combine_kernel, rt=rt, nt=nt),
        grid_spec=pltpu.PrefetchScalarGridSpec(
            num_scalar_prefetch=1,
            grid=(E, nt),
            in_specs=[pl.BlockSpec((1, rt, D), lambda e, t, idx: (e, t, 0)),
                      pl.BlockSpec(memory_space=pl.ANY)],
            out_specs=pl.BlockSpec(memory_space=pl.ANY),
            scratch_shapes=[pltpu.VMEM((2, rt, D), F32), pltpu.SemaphoreType.DMA((2,)),
                            pltpu.SemaphoreType.DMA((2,))]),
        out_shape=jax.ShapeDtypeStruct((T, D), F32),
        input_output_aliases={2: 0},
        compiler_params=_params("arbitrary", "arbitrary"),
    )(idx, ye, x)


def _moe_kernel(*refs, nj, has_ctx):
    it = iter(refs)
    xs_ref, tw_ref, gate_ref = next(it), next(it), next(it)
    xc_ref, twc_ref, gatec_ref = (next(it), next(it), next(it)) if has_ctx else (None, None, None)
    wg_ref, wu_ref, wd_ref = next(it), next(it), next(it)
    ye_ref = next(it)
    yc_ref = next(it) if has_ctx else None
    hid_scr = next(it)
    hidc_scr = next(it) if has_ctx else None
    s = pl.program_id(1)
    tf = wg_ref.shape[3]

    def row_chunks(n_rows):
        step = min(MOE_ROW_CHUNK, n_rows)
        return [slice(m * step, (m + 1) * step) for m in range(n_rows // step)]

    def up(x_ref, h_scr):
        wg = wg_ref[0, 0].astype(BF16)
        wu = wu_ref[0, 0].astype(BF16)
        for rs in row_chunks(x_ref.shape[1]):
            x = x_ref[0, rs]
            a = _dot(x, wg)
            h_scr[s, rs] = ((a * jax.nn.sigmoid(a)) * _dot(x, wu)).astype(BF16)

    def down(h_scr, y_ref, w_ref, g_ref):
        wd = wd_ref[0, 0].astype(BF16)
        for rs in row_chunks(y_ref.shape[1]):
            acc = _dot(h_scr[0, rs], wd[0:tf])
            for j in range(1, nj):
                acc = acc + _dot(h_scr[j, rs], wd[j * tf:(j + 1) * tf])
            y_ref[0, rs] = acc * w_ref[0, rs] * g_ref[...]

    @pl.when(s < nj)
    def _():
        up(xs_ref, hid_scr)
        if has_ctx:
            up(xc_ref, hidc_scr)

    @pl.when(s >= nj)
    def _():
        down(hid_scr, ye_ref, tw_ref, gate_ref)
        if has_ctx:
            down(hidc_scr, yc_ref, twc_ref, gatec_ref)


def _moe(xs, tw, gate, layer, w_gate, w_up, w_down, ctx=None):
    E, cap, D = xs.shape
    FF = w_gate.shape[3]
    tf = tn = MOE_TILE
    nj, nn = FF // tf, D // tn
    has_ctx = ctx is not None
    jj = lambda s: jnp.minimum(s, nj - 1)
    nc = lambda s: jnp.maximum(s - nj, 0)
    gvec = pl.BlockSpec((1, tn), lambda e, s: (0, nc(s)))
    args = [xs, tw, gate]
    specs = [pl.BlockSpec((1, cap, D), lambda e, s: (e, 0, 0)),
             pl.BlockSpec((1, cap, 1), lambda e, s: (e, 0, 0)), gvec]
    outs = [pl.BlockSpec((1, cap, tn), lambda e, s: (e, 0, nc(s)))]
    shapes = [jax.ShapeDtypeStruct((E, cap, D), F32)]
    scratch = [pltpu.VMEM((nj, cap, tf), BF16)]
    if has_ctx:
        xc, twc, gatec = ctx
        cc = xc.shape[1]
        args += [xc, twc, gatec]
        specs += [pl.BlockSpec((1, cc, D), lambda e, s: (e, 0, 0)),
                  pl.BlockSpec((1, cc, 1), lambda e, s: (e, 0, 0)), gvec]
        outs.append(pl.BlockSpec((1, cc, tn), lambda e, s: (e, 0, nc(s))))
        shapes.append(jax.ShapeDtypeStruct((E, cc, D), F32))
        scratch.append(pltpu.VMEM((nj, cc, tf), BF16))
    args += [w_gate, w_up, w_down]
    specs += [pl.BlockSpec((1, 1, D, tf), lambda e, s: (layer, e, 0, jj(s))),
              pl.BlockSpec((1, 1, D, tf), lambda e, s: (layer, e, 0, jj(s))),
              pl.BlockSpec((1, 1, FF, tn), lambda e, s: (layer, e, 0, nc(s)))]
    res = pl.pallas_call(
        functools.partial(_moe_kernel, nj=nj, has_ctx=has_ctx),
        grid=(E, nj + nn),
        in_specs=specs,
        out_specs=outs,
        out_shape=shapes,
        scratch_shapes=scratch,
        compiler_params=_params("arbitrary", "arbitrary"),
    )(*args)
    return res if has_ctx else (res[0], None)


def _route(aff, x, g2, shift, scale):
    cap = CAPACITY * x.shape[0] // N_EXPERTS
    top_w, top_idx = lax.top_k(aff[:, :N_EXPERTS].T, cap)
    return _gather_rows(x, top_idx, g2, shift, scale), top_w[..., None], top_idx


def _reorder_w_in(w):
    D = w.shape[0]
    sizes = (GLA_QK, GLA_QK, GLA_V, GLA_V, GLA_GATE_RANK, GLA_GATE_RANK, NA_W, NA_W, NA_W, LRU_WIDTH, LRU_WIDTH)
    parts, o = [], 0
    for s in sizes:
        parts.append(w[:, o:o + s])
        o += s
    gq, gk, gv, gr, lf, lb, nq, nk, nv, lx, ly = parts
    pad = jnp.zeros((D, LANES - 2 * GLA_GATE_RANK), w.dtype)
    main = jnp.concatenate([nq, nk, nv, gv, gr, lx, ly, gq, gk], axis=1).astype(BF16)
    return main, jnp.concatenate([lf, lb, pad], axis=1).astype(BF16)


def kernel(x, c, ctx, c_ctx, norm1_g, norm2_g, w_mod, b_mod, w_in, gla_w_gate, gla_b_gate, gla_norm_g, na_q_norm_g, na_k_norm_g, na_rpb, lru_conv_w, lru_conv_b, lru_w_a, lru_b_a, lru_w_i, lru_b_i, lru_lambda, w_out, w_router, w_exp_gate, w_exp_up, w_exp_down):
    B, T, D = x.shape
    Lc = ctx.shape[1]
    depth = w_in.shape[0]
    assert B == 1 and T % (NA_ROWS_PER_STEP * GRID_W) == 0 and T // GRID_W >= NA_UNION_ROWS
    xs = x[0]
    cs = ctx[0]
    mod = _modulation(jnp.stack([c[0], c_ctx]), w_mod, b_mod)
    cos, sin = _rope_tables(T)
    tb_l, tb_c = min(512, T), min(512, Lc)
    tri = {(tb, rev): _tri_matrix(tb, rev) for tb in {tb_l, tb_c} for rev in (False, True)}
    row = lambda v: v.reshape(1, -1)

    for l in range(depth):
        ctx_out = l < depth - 1
        mx = [row(m) for m in jnp.split(mod[l, 0], 6)]
        mc = [row(m) for m in jnp.split(mod[l, 1], 6)]
        w_main, w_lr = _reorder_w_in(w_in[l])
        g1 = row(norm1_g[l])
        zx, zx_lr = _inproj(xs, g1, mx[0], mx[1], w_main, w_lr)
        zc, zc_lr = _inproj(cs, g1, mc[0], mc[1], w_main, w_lr)

        gla_dir = []
        for d in range(2):
            wg = jnp.zeros((LANES, GLA_QK), F32).at[d * GLA_GATE_RANK:(d + 1) * GLA_GATE_RANK].set(gla_w_gate[l, d])
            gla_dir.append((wg, row(gla_b_gate[l, d])))
        ng = row(gla_norm_g[l])
        zero_state = jnp.zeros((GLA_HEADS, GLA_DV, GLA_DK), F32)
        oc_f, sc_f = _gla_pass(zc, zc_lr, None, None, *gla_dir[0], tri[(tb_c, False)], zero_state, reverse=False)
        gla_c, sc_b = _gla_pass(zc, zc_lr, None, None, *gla_dir[1], tri[(tb_c, True)], zero_state, reverse=True,
                                oprev=oc_f, norm_g=ng)
        ol_f, _ = _gla_pass(zx, zx_lr, cos, sin, *gla_dir[0], tri[(tb_l, False)], sc_f, reverse=False)
        gla_x, _ = _gla_pass(zx, zx_lr, cos, sin, *gla_dir[1], tri[(tb_l, True)], sc_b, reverse=True,
                             oprev=ol_f, norm_g=ng)

        qg, kg = row(na_q_norm_g[l]), row(na_k_norm_g[l])
        ql, kl, vl = _na_prep(zx, qg, kg)
        qc, kc, vc = _na_prep(zc, qg, kg)
        na_x = _na_latent(ql, kl, vl, kc, vc, _na_bias(na_rpb[l], T // GRID_W))

        lru_dir = []
        for d in range(2):
            lru_dir.append(dict(conv_w=lru_conv_w[l], conv_b=row(lru_conv_b[l]),
                                w_a=lru_w_a[l, d].astype(BF16), b_a=row(lru_b_a[l, d]),
                                w_i=lru_w_i[l, d].astype(BF16), b_i=row(lru_b_i[l, d]),
                                lam=row(lru_lambda[l, d])))
        zero_h = jnp.zeros((1, LRU_WIDTH), F32)
        hc_f, fin_f = _lru_pass(zc, lru_dir[0], zero_h, reverse=False)
        lru_c, fin_b = _lru_pass(zc, lru_dir[1], zero_h, reverse=True, hprev=hc_f)
        hl_f, _ = _lru_pass(zx, lru_dir[0], fin_f[0:1], reverse=False)
        lru_x, _ = _lru_pass(zx, lru_dir[1], fin_b[0:1], reverse=True, hprev=hl_f)

        wo = w_out[l].astype(BF16)
        wr = jnp.zeros((D, LANES), BF16).at[:, :N_EXPERTS].set(w_router[l].astype(BF16))
        g2 = row(norm2_g[l])
        x1, affx = _outproj(gla_x, na_x, lru_x, xs, wo, mx[2], g2, mx[3], mx[4], wr)
        xg, twx, idxx = _route(affx, x1, g2, mx[3], mx[4])
        if ctx_out:
            na_c = _na_context(qc, kc, vc)
            c1, affc = _outproj(gla_c, na_c, lru_c, cs, wo, mc[2], g2, mc[3], mc[4], wr)
            cg, twc, idxc = _route(affc, c1, g2, mc[3], mc[4])
            ye, yc = _moe(xg, twx, mx[5], l, w_exp_gate, w_exp_up, w_exp_down, ctx=(cg, twc, mc[5]))
            cs = _combine(c1, yc, idxc)
        else:
            ye, _ = _moe(xg, twx, mx[5], l, w_exp_gate, w_exp_up, w_exp_down)
        xs = _combine(x1, ye, idxx)
    return xs[None]
```

```python
import functools

import numpy as np
import jax
import jax.numpy as jnp
from jax import lax
from jax.experimental import pallas as pl
from jax.experimental.pallas import tpu as pltpu

F32 = jnp.float32
BF16 = jnp.bfloat16

GRID_W = 64
EPS = 1e-6
GLA_HEADS = 4
GLA_DK = 64
GLA_DV = 128
GLA_GATE_RANK = 16
GLA_TAU = 16.0
GLA_CHUNK = 64
ROPE_BASE = 10000.0
NA_HEADS = 8
NA_HD = 128
NA_KH = 8
NA_KW = 16
LRU_WIDTH = 512
LRU_BLOCKS = 4
LRU_CONV = 4
LRU_C = 8.0
N_EXPERTS = 16
CAPACITY = 2

GLA_QK = GLA_HEADS * GLA_DK
GLA_V = GLA_HEADS * GLA_DV
NA_W = NA_HEADS * NA_HD

LANES = 128
SUBLANES = 8
VMEM_LIMIT = 56 * 1024 * 1024
NEG = -1e30
ROW_TILE = 256
ISSUE_UNROLL = 8
MOE_TILE = 256
MOE_ROW_CHUNK = 512
LRU_HALO = 16
NA_ROWS_PER_STEP = 8
NA_UNION_ROWS = 16
NA_Q_CHUNK = 256

Z_NAQ, Z_NAK, Z_NAV = 0, NA_W, 2 * NA_W
Z_GV = 3 * NA_W
Z_GR = Z_GV + GLA_V
Z_LX = Z_GR + GLA_V
Z_LY = Z_LX + LRU_WIDTH
Z_GQ = Z_LY + LRU_WIDTH
Z_GK = Z_GQ + GLA_QK
Z_COLS = Z_GK + GLA_QK
Z_CHUNK = 512


def _params(*sem):
    return pltpu.CompilerParams(dimension_semantics=sem, vmem_limit_bytes=VMEM_LIMIT)


def _dot(a, b):
    return jnp.dot(a, b, preferred_element_type=F32)


def _dot_nt(a, b):
    return lax.dot_general(a, b, (((1,), (1,)), ((), ())), preferred_element_type=F32)


def _dot_tn(a, b):
    return lax.dot_general(a, b, (((0,), (0,)), ((), ())), preferred_element_type=F32)


def _split2(a):
    hi = a.astype(BF16)
    lo = (a - hi.astype(F32)).astype(BF16)
    return hi, lo


def _dot_hi(a, b):
    ah, al = _split2(a)
    bh, bl = _split2(b)
    return _dot(ah, bh) + _dot(al, bh) + _dot(ah, bl)


def _rms(x):
    return x * lax.rsqrt(jnp.mean(x * x, axis=-1, keepdims=True) + EPS)


def _norm2(x, g_ref, sh_ref, sc_ref):
    return (_rms(x) * g_ref[...]) * (1.0 + sc_ref[...]) + sh_ref[...]


def _load_tokens(ref, n):
    S = ref.shape[0] // n
    return jnp.concatenate([ref[pl.ds(g, n, stride=S), :] for g in range(S)], axis=1)


def _store_tokens(ref, val):
    n = val.shape[0]
    S = ref.shape[0] // n
    for g in range(S):
        ref[pl.ds(g, n, stride=S), :] = val[:, g * LANES:(g + 1) * LANES]


def _log_sigmoid(z):
    return jnp.minimum(z, 0.0) - jnp.log1p(jnp.exp(-jnp.abs(z)))


def _mod_kernel(cb_ref, w_ref, b_ref, o_ref):
    tn = w_ref.shape[2]
    s = [cb_ref[v] * jax.nn.sigmoid(cb_ref[v]) for v in range(2)]
    for j in range(tn // LANES):
        cs = slice(j * LANES, (j + 1) * LANES)
        wj = w_ref[0, :, cs]
        for v in range(2):
            o_ref[0, v:v + 1, cs] = jnp.sum(wj * s[v], axis=0, keepdims=True) + b_ref[0, :, cs]


def _modulation(c2, w_mod, b_mod):
    L, D, N6 = w_mod.shape
    tn = 1024
    cb = jnp.broadcast_to(c2[:, :, None], (2, D, LANES))
    return pl.pallas_call(
        _mod_kernel,
        grid=(L, N6 // tn),
        in_specs=[pl.BlockSpec((2, D, LANES), lambda l, j: (0, 0, 0)),
                  pl.BlockSpec((1, D, tn), lambda l, j: (l, 0, j)),
                  pl.BlockSpec((1, 1, tn), lambda l, j: (l, 0, j))],
        out_specs=pl.BlockSpec((1, 2, tn), lambda l, j: (l, 0, j)),
        out_shape=jax.ShapeDtypeStruct((L, 2, N6), F32),
        compiler_params=_params("arbitrary", "arbitrary"),
    )(cb, w_mod, b_mod.reshape(L, 1, N6))


def _inproj_kernel(x_ref, g_ref, sh_ref, sc_ref, w_ref, wlr_ref, o_ref, olr_ref):
    h = _norm2(_load_tokens(x_ref, o_ref.shape[0]), g_ref, sh_ref, sc_ref).astype(BF16)
    for n in range(w_ref.shape[1] // Z_CHUNK):
        cs = slice(n * Z_CHUNK, (n + 1) * Z_CHUNK)
        o_ref[:, cs] = _dot(h, w_ref[:, cs]).astype(BF16)
    olr_ref[...] = _dot(h, wlr_ref[...])


def _inproj(x, gain, shift, scale, w, wlr):
    D, NZ = w.shape
    S = D // LANES
    T = x.shape[0] // S
    tm = min(512, T)
    vec = pl.BlockSpec((1, D), lambda i: (0, 0))
    resident = lambda a: pl.BlockSpec(a.shape, lambda i: (0, 0), pipeline_mode=pl.Buffered(1))
    return pl.pallas_call(
        _inproj_kernel,
        grid=(T // tm,),
        in_specs=[pl.BlockSpec((tm * S, LANES), lambda i: (i, 0)), vec, vec, vec, resident(w), resident(wlr)],
        out_specs=[pl.BlockSpec((tm, NZ), lambda i: (i, 0)), pl.BlockSpec((tm, LANES), lambda i: (i, 0))],
        out_shape=[jax.ShapeDtypeStruct((T, NZ), BF16), jax.ShapeDtypeStruct((T, LANES), F32)],
        compiler_params=_params("arbitrary"),
    )(x, gain, shift, scale, w, wlr)


def _gla_kernel(*refs, reverse, rope, final, tb):
    it = iter(refs)
    q_ref, k_ref, v_ref, lr_ref = next(it), next(it), next(it), next(it)
    cos_ref, sin_ref = (next(it), next(it)) if rope else (None, None)
    wg_ref, bg_ref, tri_ref, s0_ref = next(it), next(it), next(it), next(it)
    oprev_ref, r_ref, ng_ref = (next(it), next(it), next(it)) if final else (None, None, None)
    o_ref, sfin_ref, st_scr = next(it), next(it), next(it)

    @pl.when(pl.program_id(0) == 0)
    def _():
        st_scr[...] = s0_ref[...]

    q = q_ref[...].astype(F32)
    k = k_ref[...].astype(F32)
    if rope:
        lane = lax.broadcasted_iota(jnp.int32, q.shape, 1)
        first = (lane % GLA_DK) < (GLA_DK // 2)
        cos, sin = cos_ref[...], sin_ref[...]

        def rot(t):
            return jnp.where(first, pltpu.roll(t, GLA_QK - GLA_DK // 2, 1), pltpu.roll(t, GLA_DK // 2, 1))

        q = q * cos + rot(q) * sin
        k = k * cos + rot(k) * sin
    q = q * (GLA_DK ** -0.5)

    g = _log_sigmoid(_dot_hi(lr_ref[...], wg_ref[...]) + bg_ref[...]) * (1.0 / GLA_TAU)
    g_hi = g.astype(BF16)
    g_r1 = g - g_hi.astype(F32)
    g_mid = g_r1.astype(BF16)
    g_lo = (g_r1 - g_mid.astype(F32)).astype(BF16)
    tri = tri_ref[...]
    gc = _dot(tri, g_hi) + _dot(tri, g_mid) + _dot(tri, g_lo)

    C = GLA_CHUNK
    ri = lax.broadcasted_iota(jnp.int32, (C, C), 0)
    ci = lax.broadcasted_iota(jnp.int32, (C, C), 1)
    causal = (ci >= ri) if reverse else (ci <= ri)
    nchunk = tb // C
    for c in (range(nchunk - 1, -1, -1) if reverse else range(nchunk)):
        sl = slice(c * C, (c + 1) * C)
        gcc = gc[sl]
        gt = gcc[0:1] if reverse else gcc[C - 1:C]
        qe = q[sl] * jnp.exp(gcc)
        ke = k[sl] * jnp.exp(-gcc)
        kd = k[sl] * jnp.exp(gt - gcc)
        dec = jnp.exp(gt)
        for h in range(GLA_HEADS):
            hs = slice(h * GLA_DK, (h + 1) * GLA_DK)
            vs = slice(h * GLA_DV, (h + 1) * GLA_DV)
            qh = qe[:, hs].astype(BF16)
            vh = v_ref[sl, vs]
            att = jnp.where(causal, _dot_nt(qh, ke[:, hs].astype(BF16)), 0.0)
            st = st_scr[h]
            o = _dot(att.astype(BF16), vh) + _dot_nt(qh, st.astype(BF16))
            st_scr[h] = st * dec[:, hs] + _dot_tn(vh, kd[:, hs].astype(BF16))
            if final:
                y = _rms(oprev_ref[sl, vs] + o) * ng_ref[...]
                rg = r_ref[sl, vs].astype(F32)
                o_ref[sl, vs] = (y * (rg * jax.nn.sigmoid(rg))).astype(o_ref.dtype)
            else:
                o_ref[sl, vs] = o
    sfin_ref[...] = st_scr[...]


def _gla_pass(z, zlr, cos, sin, wg, bg, tri, s0, *, reverse, oprev=None, norm_g=None):
    T = z.shape[0]
    tb = tri.shape[0]
    nb = T // tb
    rope = cos is not None
    final = oprev is not None
    blk = (lambda i: nb - 1 - i) if reverse else (lambda i: i)

    def col(width, start):
        return pl.BlockSpec((tb, width), lambda i: (blk(i), start // width))

    const2 = lambda shape: pl.BlockSpec(shape, lambda i: (0, 0))
    state = pl.BlockSpec((GLA_HEADS, GLA_DV, GLA_DK), lambda i: (0, 0, 0))
    args = [z, z, z, zlr]
    specs = [col(GLA_QK, Z_GQ), col(GLA_QK, Z_GK), col(GLA_V, Z_GV), col(LANES, 0)]
    if rope:
        args += [cos, sin]
        specs += [col(GLA_QK, 0), col(GLA_QK, 0)]
    args += [wg, bg, tri, s0]
    specs += [const2((LANES, GLA_QK)), const2((1, GLA_QK)), const2((tb, tb)), state]
    if final:
        args += [oprev, z, norm_g]
        specs += [col(GLA_V, 0), col(GLA_V, Z_GR), const2((1, GLA_DV))]
    return pl.pallas_call(
        functools.partial(_gla_kernel, reverse=reverse, rope=rope, final=final, tb=tb),
        grid=(nb,),
        in_specs=specs,
        out_specs=[col(GLA_V, 0), state],
        out_shape=[jax.ShapeDtypeStruct((T, GLA_V), BF16 if final else F32),
                   jax.ShapeDtypeStruct((GLA_HEADS, GLA_DV, GLA_DK), F32)],
        scratch_shapes=[pltpu.VMEM((GLA_HEADS, GLA_DV, GLA_DK), F32)],
        compiler_params=_params("arbitrary"),
    )(*args)


def _tri_matrix(tb, reverse):
    i = np.arange(tb)
    same = (i[:, None] // GLA_CHUNK) == (i[None, :] // GLA_CHUNK)
    order = (i[None, :] >= i[:, None]) if reverse else (i[None, :] <= i[:, None])
    return jnp.asarray(same & order, BF16)


def _rope_tables(T):
    pos = jnp.arange(T)
    row = (pos // GRID_W).astype(F32)
    col = (pos % GRID_W).astype(F32)
    nf = GLA_DK // 4
    inv = ROPE_BASE ** (-jnp.arange(nf, dtype=F32) / nf)
    ang = jnp.concatenate([row[:, None] * inv, col[:, None] * inv], axis=-1)
    cos, sin = jnp.cos(ang), jnp.sin(ang)
    cos_full = jnp.tile(jnp.concatenate([cos, cos], axis=-1), (1, GLA_HEADS))
    sin_full = jnp.tile(jnp.concatenate([-sin, sin], axis=-1), (1, GLA_HEADS))
    return cos_full, sin_full


def _naprep_kernel(q_ref, k_ref, v_ref, qg_ref, kg_ref, qo_ref, ko_ref, vo_ref):
    for h in range(NA_HEADS):
        hs = slice(h * NA_HD, (h + 1) * NA_HD)
        qo_ref[h] = (_rms(q_ref[:, hs].astype(F32)) * qg_ref[...] * (NA_HD ** -0.5)).astype(BF16)
        ko_ref[h] = (_rms(k_ref[:, hs].astype(F32)) * kg_ref[...]).astype(BF16)
        vo_ref[h] = v_ref[:, hs]


def _na_prep(z, qg, kg):
    T = z.shape[0]
    tm = min(512, T)
    col = lambda c: pl.BlockSpec((tm, NA_W), lambda i: (i, c))
    vec = pl.BlockSpec((1, NA_HD), lambda i: (0, 0))
    out = pl.BlockSpec((NA_HEADS, tm, NA_HD), lambda i: (0, i, 0))
    shp = jax.ShapeDtypeStruct((NA_HEADS, T, NA_HD), BF16)
    return pl.pallas_call(
        _naprep_kernel,
        grid=(T // tm,),
        in_specs=[col(Z_NAQ // NA_W), col(Z_NAK // NA_W), col(Z_NAV // NA_W), vec, vec],
        out_specs=[out, out, out],
        out_shape=[shp, shp, shp],
        compiler_params=_params("arbitrary"),
    )(z, z, z, qg, kg)


def _na_kernel(q_ref, k_ref, v_ref, kc_ref, vc_ref, bias_ref, o_ref, *, n_rows):
    i = pl.program_id(1)
    W = GRID_W
    first_row = jnp.clip(i * NA_ROWS_PER_STEP - NA_KH // 2, 0, n_rows - NA_UNION_ROWS)
    start = pl.multiple_of(first_row * W, W)
    kw = k_ref[0, pl.ds(start, NA_UNION_ROWS * W), :]
    vw = v_ref[0, pl.ds(start, NA_UNION_ROWS * W), :]
    kc = kc_ref[0]
    vc = vc_ref[0]
    nq = q_ref.shape[1]
    step = min(NA_Q_CHUNK, nq)
    for c in range(nq // step):
        qs = slice(c * step, (c + 1) * step)
        q = q_ref[0, qs, :]
        s_loc = _dot_nt(q, kw) + bias_ref[0, 0, qs, :]
        s_ctx = _dot_nt(q, kc)
        m = jnp.maximum(jnp.max(s_loc, axis=-1, keepdims=True), jnp.max(s_ctx, axis=-1, keepdims=True))
        p_loc = jnp.exp(s_loc - m)
        p_ctx = jnp.exp(s_ctx - m)
        den = jnp.sum(p_loc, axis=-1, keepdims=True) + jnp.sum(p_ctx, axis=-1, keepdims=True)
        o = _dot(p_loc.astype(BF16), vw) + _dot(p_ctx.astype(BF16), vc)
        o_ref[qs, :] = (o / den).astype(BF16)


def _na_latent(q, k, v, kc, vc, bias):
    H, T, hd = q.shape
    Lc = kc.shape[1]
    n_rows = T // GRID_W
    R = NA_ROWS_PER_STEP
    nblk = n_rows // R
    whole = lambda n: pl.BlockSpec((1, n, hd), lambda h, i: (h, 0, 0))
    variant = lambda i: jnp.where(i == 0, 0, jnp.where(i == nblk - 1, 2, 1))
    return pl.pallas_call(
        functools.partial(_na_kernel, n_rows=n_rows),
        grid=(H, nblk),
        in_specs=[pl.BlockSpec((1, R * GRID_W, hd), lambda h, i: (h, i, 0)),
                  whole(T), whole(T), whole(Lc), whole(Lc),
                  pl.BlockSpec((1, 1, R * GRID_W, NA_UNION_ROWS * GRID_W), lambda h, i: (h, variant(i), 0, 0))],
        out_specs=pl.BlockSpec((R * GRID_W, hd), lambda h, i: (i, h)),
        out_shape=jax.ShapeDtypeStruct((T, H * hd), BF16),
        compiler_params=_params("arbitrary", "arbitrary"),
    )(q, k, v, kc, vc, bias)


def _ctxattn_kernel(q_ref, k_ref, v_ref, o_ref):
    s = _dot_nt(q_ref[0], k_ref[0])
    p = jnp.exp(s - jnp.max(s, axis=-1, keepdims=True))
    o = _dot(p.astype(BF16), v_ref[0]) / jnp.sum(p, axis=-1, keepdims=True)
    o_ref[...] = o.astype(BF16)


def _na_context(q, k, v):
    H, Lc, hd = q.shape
    spec = pl.BlockSpec((1, Lc, hd), lambda h: (h, 0, 0))
    return pl.pallas_call(
        _ctxattn_kernel,
        grid=(H,),
        in_specs=[spec, spec, spec],
        out_specs=pl.BlockSpec((Lc, hd), lambda h: (0, h)),
        out_shape=jax.ShapeDtypeStruct((Lc, H * hd), BF16),
        compiler_params=_params("arbitrary"),
    )(q, k, v)


def _na_bias(rpb, n_rows):
    R, U, W = NA_ROWS_PER_STEP, NA_UNION_ROWS, GRID_W
    cols = np.arange(W)
    col_start = np.clip(cols - NA_KW // 2, 0, W - NA_KW)
    col_ok = (cols[None, :] >= col_start[:, None]) & (cols[None, :] < col_start[:, None] + NA_KW)
    dc = np.clip(cols[None, :] - cols[:, None] + (NA_KW - 1), 0, 2 * NA_KW - 2)
    tab = jnp.where(col_ok[None, None], rpb[:, :, dc], NEG)
    didx = np.zeros((3, R, U), np.int32)
    row_ok = np.zeros((3, R, U), bool)
    for v, r0 in enumerate((0, R, n_rows - R)):
        first_row = np.clip(r0 - NA_KH // 2, 0, n_rows - U)
        for rr in range(R):
            r = r0 + rr
            rs = np.clip(r - NA_KH // 2, 0, n_rows - NA_KH)
            kr = first_row + np.arange(U)
            row_ok[v, rr] = (kr >= rs) & (kr < rs + NA_KH)
            didx[v, rr] = np.clip(kr - r + (NA_KH - 1), 0, 2 * NA_KH - 2)
    slab = jnp.where(row_ok[None, :, :, :, None, None], tab[:, didx], NEG)
    slab = slab.transpose(0, 1, 2, 4, 3, 5)
    return slab.reshape(rpb.shape[0], 3, R * W, U * W).astype(F32)


def _lru_kernel(*refs, reverse, final, tb, nblk):
    it = iter(refs)
    x_ref, xp_ref, xn_ref = next(it), next(it), next(it)
    cw_ref, cb_ref, wa_ref, ba_ref, wi_ref, bi_ref, lam_ref, h0_ref = (next(it) for _ in range(8))
    hprev_ref, y_ref = (next(it), next(it)) if final else (None, None)
    o_ref, hfin_ref = next(it), next(it)
    xx_scr, a_scr, b_scr, carry_scr = next(it), next(it), next(it), next(it)

    i = pl.program_id(0)
    blk = (nblk - 1 - i) if reverse else i
    S = SUBLANES
    Wd = LRU_WIDTH

    @pl.when(i == 0)
    def _():
        carry_scr[...] = jnp.broadcast_to(h0_ref[...], (S, Wd))

    H = LRU_HALO
    xx_scr[0:H] = jnp.where(blk > 0, xp_ref[...].astype(F32), 0.0)
    xx_scr[H:H + tb] = x_ref[...].astype(F32)
    xx_scr[H + tb:2 * H + tb] = jnp.where(blk < nblk - 1, xn_ref[...].astype(F32), 0.0)
    xc = cb_ref[...]
    for j in range(LRU_CONV):
        xc = xc + cw_ref[j:j + 1] * xx_scr[H - 1 + j:H - 1 + j + tb]

    bw = Wd // LRU_BLOCKS
    ra, ri = [], []
    for n in range(LRU_BLOCKS):
        xb = xc[:, n * bw:(n + 1) * bw].astype(BF16)
        ra.append(_dot(xb, wa_ref[n]))
        ri.append(_dot(xb, wi_ref[n]))
    rg = jax.nn.sigmoid(jnp.concatenate(ra, axis=1) + ba_ref[...])
    ig = jax.nn.sigmoid(jnp.concatenate(ri, axis=1) + bi_ref[...])
    log_a = LRU_C * rg * _log_sigmoid(lam_ref[...])
    a = jnp.exp(log_a)
    a_scr[...] = a
    b_scr[...] = jnp.sqrt(-jnp.tanh(log_a) * (a * a + 1.0)) * (ig * xc)

    row = lax.broadcasted_iota(jnp.int32, (S, Wd), 0)
    ng = tb // S

    def body(gi, carry):
        g = (ng - 1 - gi) if reverse else gi
        off = pl.multiple_of(g * S, S)
        a = a_scr[pl.ds(off, S), :]
        b = b_scr[pl.ds(off, S), :]
        for s in (1, 2, 4):
            sh = (S - s) if reverse else s
            keep = (row < S - s) if reverse else (row >= s)
            b = jnp.where(keep, a * pltpu.roll(b, sh, 0) + b, b)
            a = jnp.where(keep, a * pltpu.roll(a, sh, 0), a)
        h = b + a * carry
        b_scr[pl.ds(off, S), :] = h
        return jnp.broadcast_to(h[0:1] if reverse else h[S - 1:S], (S, Wd))

    carry = lax.fori_loop(0, ng, body, carry_scr[...])
    carry_scr[...] = carry
    hfin_ref[...] = carry
    if final:
        o_ref[...] = ((hprev_ref[...] + b_scr[...]) * jax.nn.gelu(y_ref[...].astype(F32))).astype(o_ref.dtype)
    else:
        o_ref[...] = b_scr[...]


def _lru_pass(z, p, h0, *, reverse, hprev=None):
    T = z.shape[0]
    tb = min(512, T)
    nb = T // tb
    S = SUBLANES
    Wd = LRU_WIDTH
    final = hprev is not None
    blk = (lambda i: nb - 1 - i) if reverse else (lambda i: i)
    H = LRU_HALO
    gpb = tb // H
    xcol = Z_LX // Wd
    row = pl.BlockSpec((1, Wd), lambda i: (0, 0))
    wspec = pl.BlockSpec((LRU_BLOCKS, Wd // LRU_BLOCKS, Wd // LRU_BLOCKS), lambda i: (0, 0, 0))
    args = [z, z, z, p["conv_w"], p["conv_b"], p["w_a"], p["b_a"], p["w_i"], p["b_i"], p["lam"], h0]
    specs = [pl.BlockSpec((tb, Wd), lambda i: (blk(i), xcol)),
             pl.BlockSpec((H, Wd), lambda i: (jnp.maximum(blk(i) * gpb - 1, 0), xcol)),
             pl.BlockSpec((H, Wd), lambda i: (jnp.minimum((blk(i) + 1) * gpb, T // H - 1), xcol)),
             pl.BlockSpec((LRU_CONV, Wd), lambda i: (0, 0)), row, wspec, row, wspec, row, row, row]
    if final:
        args += [hprev, z]
        specs += [pl.BlockSpec((tb, Wd), lambda i: (blk(i), 0)),
                  pl.BlockSpec((tb, Wd), lambda i: (blk(i), Z_LY // Wd))]
    return pl.pallas_call(
        functools.partial(_lru_kernel, reverse=reverse, final=final, tb=tb, nblk=nb),
        grid=(nb,),
        in_specs=specs,
        out_specs=[pl.BlockSpec((tb, Wd), lambda i: (blk(i), 0)), pl.BlockSpec((S, Wd), lambda i: (0, 0))],
        out_shape=[jax.ShapeDtypeStruct((T, Wd), BF16 if final else F32),
                   jax.ShapeDtypeStruct((S, Wd), F32)],
        scratch_shapes=[pltpu.VMEM((tb + 2 * H, Wd), F32), pltpu.VMEM((tb, Wd), F32),
                        pltpu.VMEM((tb, Wd), F32), pltpu.VMEM((S, Wd), F32)],
        compiler_params=_params("arbitrary"),
    )(*args)


def _outproj_kernel(gla_ref, na_ref, lru_ref, x_ref, wo_ref, gate_ref, g2_ref, sh_ref, sc_ref,
                    wr_ref, xo_ref, aff_ref, mix_scr):
    mix_scr[:, 0:GLA_V] = gla_ref[...]
    mix_scr[:, GLA_V:GLA_V + NA_W] = na_ref[...]
    mix_scr[:, GLA_V + NA_W:] = lru_ref[...]
    xn = _load_tokens(x_ref, mix_scr.shape[0]) + gate_ref[...] * _dot(mix_scr[...], wo_ref[...])
    _store_tokens(xo_ref, xn)
    h = _norm2(xn, g2_ref, sh_ref, sc_ref)
    logits = _dot(h.astype(BF16), wr_ref[...])
    lane = lax.broadcasted_iota(jnp.int32, logits.shape, 1)
    logits = jnp.where(lane < N_EXPERTS, logits, NEG)
    e = jnp.exp(logits - jnp.max(logits, axis=-1, keepdims=True))
    aff_ref[...] = e / jnp.sum(e, axis=-1, keepdims=True)


def _outproj(gla, na, lru, x, wo, gate, g2, shift, scale, wr):
    D = wo.shape[1]
    S = D // LANES
    T = x.shape[0] // S
    tm = min(512, T)
    rows = lambda w: pl.BlockSpec((tm, w), lambda i: (i, 0))
    toks = pl.BlockSpec((tm * S, LANES), lambda i: (i, 0))
    full = lambda a: pl.BlockSpec(a.shape, lambda i: (0, 0), pipeline_mode=pl.Buffered(1))
    vec = pl.BlockSpec((1, D), lambda i: (0, 0))
    return pl.pallas_call(
        _outproj_kernel,
        grid=(T // tm,),
        in_specs=[rows(GLA_V), rows(NA_W), rows(LRU_WIDTH), toks, full(wo), vec, vec, vec, vec, full(wr)],
        out_specs=[toks, rows(LANES)],
        out_shape=[jax.ShapeDtypeStruct((T * S, LANES), F32), jax.ShapeDtypeStruct((T, LANES), F32)],
        scratch_shapes=[pltpu.VMEM((tm, wo.shape[0]), BF16)],
        compiler_params=_params("arbitrary"),
    )(gla, na, lru, x, wo, gate, g2, shift, scale, wr)


def _gather_kernel(idx_ref, h_hbm, g2_ref, sh_ref, sc_ref, o_ref, buf, sem, *, rt, nt, ne):
    e = pl.program_id(0)
    t = pl.program_id(1)

    step = e * nt + t
    slot = step % 2

    S = buf.shape[1] // rt

    def start_rows(first, s):
        def issue(r, carry):
            tok = idx_ref[first + r]
            pltpu.make_async_copy(h_hbm.at[pl.ds(pl.multiple_of(tok * S, S), S)],
                                  buf.at[s, pl.ds(pl.multiple_of(r * S, S), S)], sem.at[s]).start()
            return carry

        lax.fori_loop(0, rt, issue, 0, unroll=ISSUE_UNROLL)

    @pl.when(step == 0)
    def _():
        start_rows(0, 0)

    @pl.when(step + 1 < ne * nt)
    def _():
        start_rows((step + 1) * rt, 1 - slot)

    pltpu.make_async_copy(h_hbm.at[pl.ds(0, rt * S)], buf.at[slot], sem.at[slot]).wait()
    o_ref[0] = _norm2(_load_tokens(buf.at[slot], rt), g2_ref, sh_ref, sc_ref).astype(BF16)


def _gather_rows(x, idx, g2, shift, scale):
    D = g2.shape[1]
    S = D // LANES
    E, cap = idx.shape
    rt = min(ROW_TILE, cap)
    nt = cap // rt
    vec = pl.BlockSpec((1, D), lambda e, t, idx: (0, 0))
    return pl.pallas_call(
        functools.partial(_gather_kernel, rt=rt, nt=nt, ne=E),
        grid_spec=pltpu.PrefetchScalarGridSpec(
            num_scalar_prefetch=1,
            grid=(E, nt),
            in_specs=[pl.BlockSpec(memory_space=pl.ANY), vec, vec, vec],
            out_specs=pl.BlockSpec((1, rt, D), lambda e, t, idx: (e, t, 0)),
            scratch_shapes=[pltpu.VMEM((2, rt * S, LANES), F32), pltpu.SemaphoreType.DMA((2,))]),
        out_shape=jax.ShapeDtypeStruct((E, cap, D), BF16),
        compiler_params=_params("arbitrary", "arbitrary"),
    )(idx.reshape(-1), x, g2, shift, scale)


def _combine_kernel(idx_ref, ye_ref, x_in, x_out, buf, gsem, ssem, *, rt, nt):
    del x_in
    e = pl.program_id(0)
    t = pl.program_id(1)
    slot = t % 2
    S = buf.shape[1] // rt

    def start_rows(tile, s, gather):
        first = (e * nt + tile) * rt

        def body(r, carry):
            tok = idx_ref[first + r]
            hbm_rows = x_out.at[pl.ds(pl.multiple_of(tok * S, S), S)]
            vmem_rows = buf.at[s, pl.ds(pl.multiple_of(r * S, S), S)]
            if gather:
                pltpu.make_async_copy(hbm_rows, vmem_rows, gsem.at[s]).start()
            else:
                pltpu.make_async_copy(vmem_rows, hbm_rows, ssem.at[s]).start()
            return carry

        lax.fori_loop(0, rt, body, 0, unroll=ISSUE_UNROLL)

    def wait_rows(s, gather):
        if gather:
            pltpu.make_async_copy(x_out.at[pl.ds(0, rt * S)], buf.at[s], gsem.at[s]).wait()
        else:
            pltpu.make_async_copy(buf.at[s], x_out.at[pl.ds(0, rt * S)], ssem.at[s]).wait()

    @pl.when(t == 0)
    def _():
        start_rows(0, 0, True)

    wait_rows(slot, True)

    @pl.when(t + 1 < nt)
    def _():
        @pl.when(t >= 1)
        def _():
            wait_rows(1 - slot, False)

        start_rows(t + 1, 1 - slot, True)

    _store_tokens(buf.at[slot], _load_tokens(buf.at[slot], rt) + ye_ref[0])
    start_rows(t, slot, False)

    @pl.when(t == nt - 1)
    def _():
        wait_rows(slot, False)
        if nt > 1:
            wait_rows(1 - slot, False)


def _combine(x, ye, idx):
    E, cap, D = ye.shape
    S = D // LANES
    rt = min(ROW_TILE, cap)
    nt = cap // rt
    return pl.pallas_call(
        functools.partial(_combine_kernel, rt=rt, nt=nt),
        grid_spec=pltpu.PrefetchScalarGridSpec(
            num_scalar_prefetch=1,
            grid=(E, nt),
            in_specs=[pl.BlockSpec((1, rt, D), lambda e, t, idx: (e, t, 0)),
                      pl.BlockSpec(memory_space=pl.ANY)],
            out_specs=pl.BlockSpec(memory_space=pl.ANY),
            scratch_shapes=[pltpu.VMEM((2, rt * S, LANES), F32), pltpu.SemaphoreType.DMA((2,)),
                            pltpu.SemaphoreType.DMA((2,))]),
        out_shape=jax.ShapeDtypeStruct(x.shape, F32),
        input_output_aliases={2: 0},
        compiler_params=_params("arbitrary", "arbitrary"),
    )(idx.reshape(-1), ye, x)


def _moe_kernel(*refs, nj, has_ctx):
    it = iter(refs)
    xs_ref, tw_ref, gate_ref = next(it), next(it), next(it)
    xc_ref, twc_ref, gatec_ref = (next(it), next(it), next(it)) if has_ctx else (None, None, None)
    wg_ref, wu_ref, wd_ref = next(it), next(it), next(it)
    ye_ref = next(it)
    yc_ref = next(it) if has_ctx else None
    hid_scr = next(it)
    hidc_scr = next(it) if has_ctx else None
    s = pl.program_id(1)
    tf = wg_ref.shape[3]

    def row_chunks(n_rows):
        step = min(MOE_ROW_CHUNK, n_rows)
        return [slice(m * step, (m + 1) * step) for m in range(n_rows // step)]

    def up(x_ref, h_scr):
        wg = wg_ref[0, 0].astype(BF16)
        wu = wu_ref[0, 0].astype(BF16)
        for rs in row_chunks(x_ref.shape[1]):
            x = x_ref[0, rs]
            a = _dot(x, wg)
            h_scr[s, rs] = ((a * jax.nn.sigmoid(a)) * _dot(x, wu)).astype(BF16)

    def down(h_scr, y_ref, w_ref, g_ref):
        wd = wd_ref[0, 0].astype(BF16)
        for rs in row_chunks(y_ref.shape[1]):
            acc = _dot(h_scr[0, rs], wd[0:tf])
            for j in range(1, nj):
                acc = acc + _dot(h_scr[j, rs], wd[j * tf:(j + 1) * tf])
            y_ref[0, rs] = acc * w_ref[0, rs] * g_ref[...]

    @pl.when(s < nj)
    def _():
        up(xs_ref, hid_scr)
        if has_ctx:
            up(xc_ref, hidc_scr)

    @pl.when(s >= nj)
    def _():
        down(hid_scr, ye_ref, tw_ref, gate_ref)
        if has_ctx:
            down(hidc_scr, yc_ref, twc_ref, gatec_ref)


def _moe(xs, tw, gate, layer, w_gate, w_up, w_down, ctx=None):
    E, cap, D = xs.shape
    FF = w_gate.shape[3]
    tf = tn = MOE_TILE
    nj, nn = FF // tf, D // tn
    has_ctx = ctx is not None
    jj = lambda s: jnp.minimum(s, nj - 1)
    nc = lambda s: jnp.maximum(s - nj, 0)
    gvec = pl.BlockSpec((1, tn), lambda e, s: (0, nc(s)))
    args = [xs, tw, gate]
    specs = [pl.BlockSpec((1, cap, D), lambda e, s: (e, 0, 0)),
             pl.BlockSpec((1, cap, 1), lambda e, s: (e, 0, 0)), gvec]
    outs = [pl.BlockSpec((1, cap, tn), lambda e, s: (e, 0, nc(s)))]
    shapes = [jax.ShapeDtypeStruct((E, cap, D), F32)]
    scratch = [pltpu.VMEM((nj, cap, tf), BF16)]
    if has_ctx:
        xc, twc, gatec = ctx
        cc = xc.shape[1]
        args += [xc, twc, gatec]
        specs += [pl.BlockSpec((1, cc, D), lambda e, s: (e, 0, 0)),
                  pl.BlockSpec((1, cc, 1), lambda e, s: (e, 0, 0)), gvec]
        outs.append(pl.BlockSpec((1, cc, tn), lambda e, s: (e, 0, nc(s))))
        shapes.append(jax.ShapeDtypeStruct((E, cc, D), F32))
        scratch.append(pltpu.VMEM((nj, cc, tf), BF16))
    args += [w_gate, w_up, w_down]
    specs += [pl.BlockSpec((1, 1, D, tf), lambda e, s: (layer, e, 0, jj(s))),
              pl.BlockSpec((1, 1, D, tf), lambda e, s: (layer, e, 0, jj(s))),
              pl.BlockSpec((1, 1, FF, tn), lambda e, s: (layer, e, 0, nc(s)))]
    res = pl.pallas_call(
        functools.partial(_moe_kernel, nj=nj, has_ctx=has_ctx),
        grid=(E, nj + nn),
        in_specs=specs,
        out_specs=outs,
        out_shape=shapes,
        scratch_shapes=scratch,
        compiler_params=_params("arbitrary", "arbitrary"),
    )(*args)
    return res if has_ctx else (res[0], None)


def _route(aff, x, g2, shift, scale):
    cap = CAPACITY * aff.shape[0] // N_EXPERTS
    top_w, top_idx = lax.top_k(aff[:, :N_EXPERTS].T, cap)
    return _gather_rows(x, top_idx, g2, shift, scale), top_w[..., None], top_idx


def _reorder_w_in(w):
    D = w.shape[0]
    sizes = (GLA_QK, GLA_QK, GLA_V, GLA_V, GLA_GATE_RANK, GLA_GATE_RANK, NA_W, NA_W, NA_W, LRU_WIDTH, LRU_WIDTH)
    parts, o = [], 0
    for s in sizes:
        parts.append(w[:, o:o + s])
        o += s
    gq, gk, gv, gr, lf, lb, nq, nk, nv, lx, ly = parts
    pad = jnp.zeros((D, LANES - 2 * GLA_GATE_RANK), w.dtype)
    main = jnp.concatenate([nq, nk, nv, gv, gr, lx, ly, gq, gk], axis=1).astype(BF16)
    return main, jnp.concatenate([lf, lb, pad], axis=1).astype(BF16)


def kernel(x, c, ctx, c_ctx, norm1_g, norm2_g, w_mod, b_mod, w_in, gla_w_gate, gla_b_gate, gla_norm_g, na_q_norm_g, na_k_norm_g, na_rpb, lru_conv_w, lru_conv_b, lru_w_a, lru_b_a, lru_w_i, lru_b_i, lru_lambda, w_out, w_router, w_exp_gate, w_exp_up, w_exp_down):
    B, T, D = x.shape
    Lc = ctx.shape[1]
    depth = w_in.shape[0]
    assert B == 1 and T % (NA_ROWS_PER_STEP * GRID_W) == 0 and T // GRID_W >= NA_UNION_ROWS
    xs = x[0].reshape(T * D // LANES, LANES)
    cs = ctx[0].reshape(Lc * D // LANES, LANES)
    mod = _modulation(jnp.stack([c[0], c_ctx]), w_mod, b_mod)
    cos, sin = _rope_tables(T)
    tb_l, tb_c = min(512, T), min(512, Lc)
    tri = {(tb, rev): _tri_matrix(tb, rev) for tb in {tb_l, tb_c} for rev in (False, True)}
    row = lambda v: v.reshape(1, -1)

    for l in range(depth):
        ctx_out = l < depth - 1
        mx = [row(m) for m in jnp.split(mod[l, 0], 6)]
        mc = [row(m) for m in jnp.split(mod[l, 1], 6)]
        w_main, w_lr = _reorder_w_in(w_in[l])
        g1 = row(norm1_g[l])
        zx, zx_lr = _inproj(xs, g1, mx[0], mx[1], w_main, w_lr)
        zc, zc_lr = _inproj(cs, g1, mc[0], mc[1], w_main, w_lr)

        gla_dir = []
        for d in range(2):
            wg = jnp.zeros((LANES, GLA_QK), F32).at[d * GLA_GATE_RANK:(d + 1) * GLA_GATE_RANK].set(gla_w_gate[l, d])
            gla_dir.append((wg, row(gla_b_gate[l, d])))
        ng = row(gla_norm_g[l])
        zero_state = jnp.zeros((GLA_HEADS, GLA_DV, GLA_DK), F32)
        oc_f, sc_f = _gla_pass(zc, zc_lr, None, None, *gla_dir[0], tri[(tb_c, False)], zero_state, reverse=False)
        gla_c, sc_b = _gla_pass(zc, zc_lr, None, None, *gla_dir[1], tri[(tb_c, True)], zero_state, reverse=True,
                                oprev=oc_f, norm_g=ng)
        ol_f, _ = _gla_pass(zx, zx_lr, cos, sin, *gla_dir[0], tri[(tb_l, False)], sc_f, reverse=False)
        gla_x, _ = _gla_pass(zx, zx_lr, cos, sin, *gla_dir[1], tri[(tb_l, True)], sc_b, reverse=True,
                             oprev=ol_f, norm_g=ng)

        qg, kg = row(na_q_norm_g[l]), row(na_k_norm_g[l])
        ql, kl, vl = _na_prep(zx, qg, kg)
        qc, kc, vc = _na_prep(zc, qg, kg)
        na_x = _na_latent(ql, kl, vl, kc, vc, _na_bias(na_rpb[l], T // GRID_W))

        lru_dir = []
        for d in range(2):
            lru_dir.append(dict(conv_w=lru_conv_w[l], conv_b=row(lru_conv_b[l]),
                                w_a=lru_w_a[l, d].astype(BF16), b_a=row(lru_b_a[l, d]),
                                w_i=lru_w_i[l, d].astype(BF16), b_i=row(lru_b_i[l, d]),
                                lam=row(lru_lambda[l, d])))
        zero_h = jnp.zeros((1, LRU_WIDTH), F32)
        hc_f, fin_f = _lru_pass(zc, lru_dir[0], zero_h, reverse=False)
        lru_c, fin_b = _lru_pass(zc, lru_dir[1], zero_h, reverse=True, hprev=hc_f)
        hl_f, _ = _lru_pass(zx, lru_dir[0], fin_f[0:1], reverse=False)
        lru_x, _ = _lru_pass(zx, lru_dir[1], fin_b[0:1], reverse=True, hprev=hl_f)

        wo = w_out[l].astype(BF16)
        wr = jnp.zeros((D, LANES), BF16).at[:, :N_EXPERTS].set(w_router[l].astype(BF16))
        g2 = row(norm2_g[l])
        x1, affx = _outproj(gla_x, na_x, lru_x, xs, wo, mx[2], g2, mx[3], mx[4], wr)
        xg, twx, idxx = _route(affx, x1, g2, mx[3], mx[4])
        if ctx_out:
            na_c = _na_context(qc, kc, vc)
            c1, affc = _outproj(gla_c, na_c, lru_c, cs, wo, mc[2], g2, mc[3], mc[4], wr)
            cg, twc, idxc = _route(affc, c1, g2, mc[3], mc[4])
            ye, yc = _moe(xg, twx, mx[5], l, w_exp_gate, w_exp_up, w_exp_down, ctx=(cg, twc, mc[5]))
            cs = _combine(c1, yc, idxc)
        else:
            ye, _ = _moe(xg, twx, mx[5], l, w_exp_gate, w_exp_up, w_exp_down)
        xs = _combine(x1, ye, idxx)
    return xs.reshape(1, T, D)
```

```python
import functools

import numpy as np
import jax
import jax.numpy as jnp
from jax import lax
from jax.experimental import pallas as pl
from jax.experimental.pallas import tpu as pltpu

F32 = jnp.float32
BF16 = jnp.bfloat16

GRID_W = 64
EPS = 1e-6
GLA_HEADS = 4
GLA_DK = 64
GLA_DV = 128
GLA_GATE_RANK = 16
GLA_TAU = 16.0
GLA_CHUNK = 64
ROPE_BASE = 10000.0
NA_HEADS = 8
NA_HD = 128
NA_KH = 8
NA_KW = 16
LRU_WIDTH = 512
LRU_BLOCKS = 4
LRU_CONV = 4
LRU_C = 8.0
N_EXPERTS = 16
CAPACITY = 2

GLA_QK = GLA_HEADS * GLA_DK
GLA_V = GLA_HEADS * GLA_DV
NA_W = NA_HEADS * NA_HD

LANES = 128
SUBLANES = 8
VMEM_LIMIT = 56 * 1024 * 1024
NEG = -1e30
ROW_TILE = 256
ISSUE_UNROLL = 8
SELECT_BISECTIONS = 40
MOE_TILE = 256
MOE_ROW_CHUNK = 512
LRU_HALO = 16
NA_ROWS_PER_STEP = 8
NA_UNION_ROWS = 16
NA_Q_CHUNK = 256
NA_PAIR_OFFSET = NA_UNION_ROWS - NA_KH
NA_PAIR_ENTRIES = NA_UNION_ROWS + 2 * NA_KH - 2

Z_NAQ, Z_NAK, Z_NAV = 0, NA_W, 2 * NA_W
Z_GV = 3 * NA_W
Z_GR = Z_GV + GLA_V
Z_LX = Z_GR + GLA_V
Z_LY = Z_LX + LRU_WIDTH
Z_GQ = Z_LY + LRU_WIDTH
Z_GK = Z_GQ + GLA_QK
Z_COLS = Z_GK + GLA_QK
Z_CHUNK = 512


def _params(*sem):
    return pltpu.CompilerParams(dimension_semantics=sem, vmem_limit_bytes=VMEM_LIMIT)


def _dot(a, b):
    return jnp.dot(a, b, preferred_element_type=F32)


def _dot_nt(a, b):
    return lax.dot_general(a, b, (((1,), (1,)), ((), ())), preferred_element_type=F32)


def _dot_tn(a, b):
    return lax.dot_general(a, b, (((0,), (0,)), ((), ())), preferred_element_type=F32)


def _split2(a):
    hi = a.astype(BF16)
    lo = (a - hi.astype(F32)).astype(BF16)
    return hi, lo


def _dot_hi(a, b):
    ah, al = _split2(a)
    bh, bl = _split2(b)
    return _dot(ah, bh) + _dot(al, bh) + _dot(ah, bl)


def _rms(x):
    return x * lax.rsqrt(jnp.mean(x * x, axis=-1, keepdims=True) + EPS)


def _norm2(x, g_ref, sh_ref, sc_ref):
    return (_rms(x) * g_ref[...]) * (1.0 + sc_ref[...]) + sh_ref[...]


def _log_sigmoid(z):
    return jnp.minimum(z, 0.0) - jnp.log1p(jnp.exp(-jnp.abs(z)))


def _mod_kernel(cb_ref, w_ref, b_ref, o_ref):
    tn = w_ref.shape[2]
    s = [cb_ref[v] * jax.nn.sigmoid(cb_ref[v]) for v in range(2)]
    for j in range(tn // LANES):
        cs = slice(j * LANES, (j + 1) * LANES)
        wj = w_ref[0, :, cs]
        for v in range(2):
            o_ref[0, v:v + 1, cs] = jnp.sum(wj * s[v], axis=0, keepdims=True) + b_ref[0, :, cs]


def _modulation(c2, w_mod, b_mod):
    L, D, N6 = w_mod.shape
    tn = 1024
    cb = jnp.broadcast_to(c2[:, :, None], (2, D, LANES))
    return pl.pallas_call(
        _mod_kernel,
        grid=(L, N6 // tn),
        in_specs=[pl.BlockSpec((2, D, LANES), lambda l, j: (0, 0, 0)),
                  pl.BlockSpec((1, D, tn), lambda l, j: (l, 0, j)),
                  pl.BlockSpec((1, 1, tn), lambda l, j: (l, 0, j))],
        out_specs=pl.BlockSpec((1, 2, tn), lambda l, j: (l, 0, j)),
        out_shape=jax.ShapeDtypeStruct((L, 2, N6), F32),
        compiler_params=_params("arbitrary", "arbitrary"),
    )(cb, w_mod, b_mod.reshape(L, 1, N6))


def _inproj_kernel(x_ref, g_ref, sh_ref, sc_ref, w_ref, wlr_ref, o_ref, olr_ref):
    h = _norm2(x_ref[...], g_ref, sh_ref, sc_ref).astype(BF16)
    for n in range(w_ref.shape[1] // Z_CHUNK):
        cs = slice(n * Z_CHUNK, (n + 1) * Z_CHUNK)
        o_ref[:, cs] = _dot(h, w_ref[:, cs]).astype(BF16)
    olr_ref[...] = _dot(h, wlr_ref[...])


def _inproj(x, gain, shift, scale, w, wlr):
    T, D = x.shape
    NZ = w.shape[1]
    tm = min(512, T)
    vec = pl.BlockSpec((1, D), lambda i: (0, 0))
    resident = lambda a: pl.BlockSpec(a.shape, lambda i: (0, 0), pipeline_mode=pl.Buffered(1))
    return pl.pallas_call(
        _inproj_kernel,
        grid=(T // tm,),
        in_specs=[pl.BlockSpec((tm, D), lambda i: (i, 0)), vec, vec, vec, resident(w), resident(wlr)],
        out_specs=[pl.BlockSpec((tm, NZ), lambda i: (i, 0)), pl.BlockSpec((tm, LANES), lambda i: (i, 0))],
        out_shape=[jax.ShapeDtypeStruct((T, NZ), BF16), jax.ShapeDtypeStruct((T, LANES), F32)],
        compiler_params=_params("arbitrary"),
    )(x, gain, shift, scale, w, wlr)


def _gla_kernel(*refs, reverse, rope, final, tb):
    it = iter(refs)
    q_ref, k_ref, v_ref, lr_ref = next(it), next(it), next(it), next(it)
    cos_ref, sin_ref = (next(it), next(it)) if rope else (None, None)
    wg_ref, bg_ref, tri_ref, s0_ref = next(it), next(it), next(it), next(it)
    oprev_ref, r_ref, ng_ref = (next(it), next(it), next(it)) if final else (None, None, None)
    o_ref, sfin_ref, st_scr = next(it), next(it), next(it)

    @pl.when(pl.program_id(0) == 0)
    def _():
        st_scr[...] = s0_ref[...]

    q = q_ref[...].astype(F32)
    k = k_ref[...].astype(F32)
    if rope:
        lane = lax.broadcasted_iota(jnp.int32, q.shape, 1)
        first = (lane % GLA_DK) < (GLA_DK // 2)
        cos, sin = cos_ref[...], sin_ref[...]

        def rot(t):
            return jnp.where(first, pltpu.roll(t, GLA_QK - GLA_DK // 2, 1), pltpu.roll(t, GLA_DK // 2, 1))

        q = q * cos + rot(q) * sin
        k = k * cos + rot(k) * sin
    q = q * (GLA_DK ** -0.5)

    g = _log_sigmoid(_dot_hi(lr_ref[...], wg_ref[...]) + bg_ref[...]) * (1.0 / GLA_TAU)
    g_hi = g.astype(BF16)
    g_r1 = g - g_hi.astype(F32)
    g_mid = g_r1.astype(BF16)
    g_lo = (g_r1 - g_mid.astype(F32)).astype(BF16)
    tri = tri_ref[...]
    gc = _dot(tri, g_hi) + _dot(tri, g_mid) + _dot(tri, g_lo)

    C = GLA_CHUNK
    ri = lax.broadcasted_iota(jnp.int32, (C, C), 0)
    ci = lax.broadcasted_iota(jnp.int32, (C, C), 1)
    causal = (ci >= ri) if reverse else (ci <= ri)
    nchunk = tb // C
    for c in (range(nchunk - 1, -1, -1) if reverse else range(nchunk)):
        sl = slice(c * C, (c + 1) * C)
        gcc = gc[sl]
        gt = gcc[0:1] if reverse else gcc[C - 1:C]
        qe = q[sl] * jnp.exp(gcc)
        ke = k[sl] * jnp.exp(-gcc)
        kd = k[sl] * jnp.exp(gt - gcc)
        dec = jnp.exp(gt)
        for h in range(GLA_HEADS):
            hs = slice(h * GLA_DK, (h + 1) * GLA_DK)
            vs = slice(h * GLA_DV, (h + 1) * GLA_DV)
            qh = qe[:, hs].astype(BF16)
            vh = v_ref[sl, vs]
            att = jnp.where(causal, _dot_nt(qh, ke[:, hs].astype(BF16)), 0.0)
            st = st_scr[h]
            o = _dot(att.astype(BF16), vh) + _dot_nt(qh, st.astype(BF16))
            st_scr[h] = st * dec[:, hs] + _dot_tn(vh, kd[:, hs].astype(BF16))
            if final:
                y = _rms(oprev_ref[sl, vs] + o) * ng_ref[...]
                rg = r_ref[sl, vs].astype(F32)
                o_ref[sl, vs] = (y * (rg * jax.nn.sigmoid(rg))).astype(o_ref.dtype)
            else:
                o_ref[sl, vs] = o
    sfin_ref[...] = st_scr[...]


def _gla_pass(z, zlr, cos, sin, wg, bg, tri, s0, *, reverse, oprev=None, norm_g=None):
    T = z.shape[0]
    tb = tri.shape[0]
    nb = T // tb
    rope = cos is not None
    final = oprev is not None
    blk = (lambda i: nb - 1 - i) if reverse else (lambda i: i)

    def col(width, start):
        return pl.BlockSpec((tb, width), lambda i: (blk(i), start // width))

    const2 = lambda shape: pl.BlockSpec(shape, lambda i: (0, 0))
    state = pl.BlockSpec((GLA_HEADS, GLA_DV, GLA_DK), lambda i: (0, 0, 0))
    args = [z, z, z, zlr]
    specs = [col(GLA_QK, Z_GQ), col(GLA_QK, Z_GK), col(GLA_V, Z_GV), col(LANES, 0)]
    if rope:
        args += [cos, sin]
        specs += [col(GLA_QK, 0), col(GLA_QK, 0)]
    args += [wg, bg, tri, s0]
    specs += [const2((LANES, GLA_QK)), const2((1, GLA_QK)), const2((tb, tb)), state]
    if final:
        args += [oprev, z, norm_g]
        specs += [col(GLA_V, 0), col(GLA_V, Z_GR), const2((1, GLA_DV))]
    return pl.pallas_call(
        functools.partial(_gla_kernel, reverse=reverse, rope=rope, final=final, tb=tb),
        grid=(nb,),
        in_specs=specs,
        out_specs=[col(GLA_V, 0), state],
        out_shape=[jax.ShapeDtypeStruct((T, GLA_V), BF16 if final else F32),
                   jax.ShapeDtypeStruct((GLA_HEADS, GLA_DV, GLA_DK), F32)],
        scratch_shapes=[pltpu.VMEM((GLA_HEADS, GLA_DV, GLA_DK), F32)],
        compiler_params=_params("arbitrary"),
    )(*args)


def _tri_matrix(tb, reverse):
    i = np.arange(tb)
    same = (i[:, None] // GLA_CHUNK) == (i[None, :] // GLA_CHUNK)
    order = (i[None, :] >= i[:, None]) if reverse else (i[None, :] <= i[:, None])
    return jnp.asarray(same & order, BF16)


def _rope_tables(T):
    pos = jnp.arange(T)
    row = (pos // GRID_W).astype(F32)
    col = (pos % GRID_W).astype(F32)
    nf = GLA_DK // 4
    inv = ROPE_BASE ** (-jnp.arange(nf, dtype=F32) / nf)
    ang = jnp.concatenate([row[:, None] * inv, col[:, None] * inv], axis=-1)
    cos, sin = jnp.cos(ang), jnp.sin(ang)
    cos_full = jnp.tile(jnp.concatenate([cos, cos], axis=-1), (1, GLA_HEADS))
    sin_full = jnp.tile(jnp.concatenate([-sin, sin], axis=-1), (1, GLA_HEADS))
    return cos_full, sin_full


def _naprep_kernel(q_ref, k_ref, v_ref, qg_ref, kg_ref, qo_ref, ko_ref, vo_ref):
    for h in range(NA_HEADS):
        hs = slice(h * NA_HD, (h + 1) * NA_HD)
        qo_ref[h] = (_rms(q_ref[:, hs].astype(F32)) * qg_ref[...] * (NA_HD ** -0.5)).astype(BF16)
        ko_ref[h] = (_rms(k_ref[:, hs].astype(F32)) * kg_ref[...]).astype(BF16)
        vo_ref[h] = v_ref[:, hs]


def _na_prep(z, qg, kg):
    T = z.shape[0]
    tm = min(512, T)
    col = lambda c: pl.BlockSpec((tm, NA_W), lambda i: (i, c))
    vec = pl.BlockSpec((1, NA_HD), lambda i: (0, 0))
    out = pl.BlockSpec((NA_HEADS, tm, NA_HD), lambda i: (0, i, 0))
    shp = jax.ShapeDtypeStruct((NA_HEADS, T, NA_HD), BF16)
    return pl.pallas_call(
        _naprep_kernel,
        grid=(T // tm,),
        in_specs=[col(Z_NAQ // NA_W), col(Z_NAK // NA_W), col(Z_NAV // NA_W), vec, vec],
        out_specs=[out, out, out],
        out_shape=[shp, shp, shp],
        compiler_params=_params("arbitrary"),
    )(z, z, z, qg, kg)


def _na_kernel(q_ref, k_ref, v_ref, kc_ref, vc_ref, bias_ref, o_ref, *, n_rows):
    i = pl.program_id(1)
    W = GRID_W
    R = NA_ROWS_PER_STEP
    first_row = jnp.clip(i * R - NA_KH // 2, 0, n_rows - NA_UNION_ROWS)
    start = pl.multiple_of(first_row * W, W)
    kw = k_ref[0, pl.ds(start, NA_UNION_ROWS * W), :]
    vw = v_ref[0, pl.ds(start, NA_UNION_ROWS * W), :]
    kc = kc_ref[0]
    vc = vc_ref[0]
    upper = lax.broadcasted_iota(jnp.int32, (1, 2 * W), 1) // W

    def bias_rows(rr):
        r = i * R + rr
        lo = jnp.clip(r - NA_KH // 2, 0, n_rows - NA_KH) - r + (NA_KH - 1)
        blocks = []
        for m in range(NA_UNION_ROWS // 2):
            a_even = first_row + 2 * m - r + (NA_KH - 1)
            a = a_even + upper
            ok = jnp.logical_and(a >= lo, a < lo + NA_KH)
            blocks.append(jnp.where(ok, bias_ref[0, a_even + NA_PAIR_OFFSET], NEG))
        return jnp.concatenate(blocks, axis=1)

    nq = q_ref.shape[1]
    step = min(NA_Q_CHUNK, nq)
    for c in range(nq // step):
        qs = slice(c * step, (c + 1) * step)
        q = q_ref[0, qs, :]
        bias = jnp.concatenate([bias_rows(rr) for rr in range(c * step // W, (c + 1) * step // W)], axis=0)
        s_loc = _dot_nt(q, kw) + bias
        s_ctx = _dot_nt(q, kc)
        m = jnp.maximum(jnp.max(s_loc, axis=-1, keepdims=True), jnp.max(s_ctx, axis=-1, keepdims=True))
        p_loc = jnp.exp(s_loc - m)
        p_ctx = jnp.exp(s_ctx - m)
        den = jnp.sum(p_loc, axis=-1, keepdims=True) + jnp.sum(p_ctx, axis=-1, keepdims=True)
        o = _dot(p_loc.astype(BF16), vw) + _dot(p_ctx.astype(BF16), vc)
        o_ref[qs, :] = (o / den).astype(BF16)


def _na_latent(q, k, v, kc, vc, bias):
    H, T, hd = q.shape
    Lc = kc.shape[1]
    n_rows = T // GRID_W
    R = NA_ROWS_PER_STEP
    nblk = n_rows // R
    whole = lambda n: pl.BlockSpec((1, n, hd), lambda h, i: (h, 0, 0))
    return pl.pallas_call(
        functools.partial(_na_kernel, n_rows=n_rows),
        grid=(H, nblk),
        in_specs=[pl.BlockSpec((1, R * GRID_W, hd), lambda h, i: (h, i, 0)),
                  whole(T), whole(T), whole(Lc), whole(Lc),
                  pl.BlockSpec((1,) + bias.shape[1:], lambda h, i: (h, 0, 0, 0))],
        out_specs=pl.BlockSpec((R * GRID_W, hd), lambda h, i: (i, h)),
        out_shape=jax.ShapeDtypeStruct((T, H * hd), BF16),
        compiler_params=_params("arbitrary", "arbitrary"),
    )(q, k, v, kc, vc, bias)


def _ctxattn_kernel(q_ref, k_ref, v_ref, o_ref):
    s = _dot_nt(q_ref[0], k_ref[0])
    p = jnp.exp(s - jnp.max(s, axis=-1, keepdims=True))
    o = _dot(p.astype(BF16), v_ref[0]) / jnp.sum(p, axis=-1, keepdims=True)
    o_ref[...] = o.astype(BF16)


def _na_context(q, k, v):
    H, Lc, hd = q.shape
    spec = pl.BlockSpec((1, Lc, hd), lambda h: (h, 0, 0))
    return pl.pallas_call(
        _ctxattn_kernel,
        grid=(H,),
        in_specs=[spec, spec, spec],
        out_specs=pl.BlockSpec((Lc, hd), lambda h: (0, h)),
        out_shape=jax.ShapeDtypeStruct((Lc, H * hd), BF16),
        compiler_params=_params("arbitrary"),
    )(q, k, v)


def _na_bias(rpb):
    W = GRID_W
    cols = np.arange(W)
    col_start = np.clip(cols - NA_KW // 2, 0, W - NA_KW)
    col_ok = (cols[None, :] >= col_start[:, None]) & (cols[None, :] < col_start[:, None] + NA_KW)
    dc = np.clip(cols[None, :] - cols[:, None] + (NA_KW - 1), 0, 2 * NA_KW - 2)
    tab = jnp.where(col_ok[None, None], rpb[:, :, dc], NEG).astype(F32)
    n_tab = 2 * NA_KH - 1
    pad_lo = jnp.full((rpb.shape[0], NA_PAIR_OFFSET, W, W), NEG, F32)
    pad_hi = jnp.full((rpb.shape[0], NA_PAIR_ENTRIES + 1 - NA_PAIR_OFFSET - n_tab, W, W), NEG, F32)
    ext = jnp.concatenate([pad_lo, tab, pad_hi], axis=1)
    return jnp.concatenate([ext[:, :-1], ext[:, 1:]], axis=-1)


def _lru_kernel(*refs, reverse, final, tb, nblk):
    it = iter(refs)
    x_ref, xp_ref, xn_ref = next(it), next(it), next(it)
    cw_ref, cb_ref, wa_ref, ba_ref, wi_ref, bi_ref, lam_ref, h0_ref = (next(it) for _ in range(8))
    hprev_ref, y_ref = (next(it), next(it)) if final else (None, None)
    o_ref, hfin_ref = next(it), next(it)
    xx_scr, a_scr, b_scr, carry_scr = next(it), next(it), next(it), next(it)

    i = pl.program_id(0)
    blk = (nblk - 1 - i) if reverse else i
    S = SUBLANES
    Wd = LRU_WIDTH

    @pl.when(i == 0)
    def _():
        carry_scr[...] = jnp.broadcast_to(h0_ref[...], (S, Wd))

    H = LRU_HALO
    xx_scr[0:H] = jnp.where(blk > 0, xp_ref[...].astype(F32), 0.0)
    xx_scr[H:H + tb] = x_ref[...].astype(F32)
    xx_scr[H + tb:2 * H + tb] = jnp.where(blk < nblk - 1, xn_ref[...].astype(F32), 0.0)
    xc = cb_ref[...]
    for j in range(LRU_CONV):
        xc = xc + cw_ref[j:j + 1] * xx_scr[H - 1 + j:H - 1 + j + tb]

    bw = Wd // LRU_BLOCKS
    ra, ri = [], []
    for n in range(LRU_BLOCKS):
        xb = xc[:, n * bw:(n + 1) * bw].astype(BF16)
        ra.append(_dot(xb, wa_ref[n]))
        ri.append(_dot(xb, wi_ref[n]))
    rg = jax.nn.sigmoid(jnp.concatenate(ra, axis=1) + ba_ref[...])
    ig = jax.nn.sigmoid(jnp.concatenate(ri, axis=1) + bi_ref[...])
    log_a = LRU_C * rg * _log_sigmoid(lam_ref[...])
    a = jnp.exp(log_a)
    a_scr[...] = a
    b_scr[...] = jnp.sqrt(-jnp.tanh(log_a) * (a * a + 1.0)) * (ig * xc)

    row = lax.broadcasted_iota(jnp.int32, (S, Wd), 0)
    ng = tb // S

    def body(gi, carry):
        g = (ng - 1 - gi) if reverse else gi
        off = pl.multiple_of(g * S, S)
        a = a_scr[pl.ds(off, S), :]
        b = b_scr[pl.ds(off, S), :]
        for s in (1, 2, 4):
            sh = (S - s) if reverse else s
            keep = (row < S - s) if reverse else (row >= s)
            b = jnp.where(keep, a * pltpu.roll(b, sh, 0) + b, b)
            a = jnp.where(keep, a * pltpu.roll(a, sh, 0), a)
        h = b + a * carry
        b_scr[pl.ds(off, S), :] = h
        return jnp.broadcast_to(h[0:1] if reverse else h[S - 1:S], (S, Wd))

    carry = lax.fori_loop(0, ng, body, carry_scr[...])
    carry_scr[...] = carry
    hfin_ref[...] = carry
    if final:
        o_ref[...] = ((hprev_ref[...] + b_scr[...]) * jax.nn.gelu(y_ref[...].astype(F32))).astype(o_ref.dtype)
    else:
        o_ref[...] = b_scr[...]


def _lru_pass(z, p, h0, *, reverse, hprev=None):
    T = z.shape[0]
    tb = min(512, T)
    nb = T // tb
    S = SUBLANES
    Wd = LRU_WIDTH
    final = hprev is not None
    blk = (lambda i: nb - 1 - i) if reverse else (lambda i: i)
    H = LRU_HALO
    gpb = tb // H
    xcol = Z_LX // Wd
    row = pl.BlockSpec((1, Wd), lambda i: (0, 0))
    wspec = pl.BlockSpec((LRU_BLOCKS, Wd // LRU_BLOCKS, Wd // LRU_BLOCKS), lambda i: (0, 0, 0))
    args = [z, z, z, p["conv_w"], p["conv_b"], p["w_a"], p["b_a"], p["w_i"], p["b_i"], p["lam"], h0]
    specs = [pl.BlockSpec((tb, Wd), lambda i: (blk(i), xcol)),
             pl.BlockSpec((H, Wd), lambda i: (jnp.maximum(blk(i) * gpb - 1, 0), xcol)),
             pl.BlockSpec((H, Wd), lambda i: (jnp.minimum((blk(i) + 1) * gpb, T // H - 1), xcol)),
             pl.BlockSpec((LRU_CONV, Wd), lambda i: (0, 0)), row, wspec, row, wspec, row, row, row]
    if final:
        args += [hprev, z]
        specs += [pl.BlockSpec((tb, Wd), lambda i: (blk(i), 0)),
                  pl.BlockSpec((tb, Wd), lambda i: (blk(i), Z_LY // Wd))]
    return pl.pallas_call(
        functools.partial(_lru_kernel, reverse=reverse, final=final, tb=tb, nblk=nb),
        grid=(nb,),
        in_specs=specs,
        out_specs=[pl.BlockSpec((tb, Wd), lambda i: (blk(i), 0)), pl.BlockSpec((S, Wd), lambda i: (0, 0))],
        out_shape=[jax.ShapeDtypeStruct((T, Wd), BF16 if final else F32),
                   jax.ShapeDtypeStruct((S, Wd), F32)],
        scratch_shapes=[pltpu.VMEM((tb + 2 * H, Wd), F32), pltpu.VMEM((tb, Wd), F32),
                        pltpu.VMEM((tb, Wd), F32), pltpu.VMEM((S, Wd), F32)],
        compiler_params=_params("arbitrary"),
    )(*args)


def _outproj_kernel(gla_ref, na_ref, lru_ref, x_ref, wo_ref, gate_ref, g2_ref, sh_ref, sc_ref,
                    wr_ref, xo_ref, aff_ref, mix_scr):
    mix_scr[:, 0:GLA_V] = gla_ref[...]
    mix_scr[:, GLA_V:GLA_V + NA_W] = na_ref[...]
    mix_scr[:, GLA_V + NA_W:] = lru_ref[...]
    xn = x_ref[...] + gate_ref[...] * _dot(mix_scr[...], wo_ref[...])
    xo_ref[...] = xn
    h = _norm2(xn, g2_ref, sh_ref, sc_ref)
    logits = _dot(h.astype(BF16), wr_ref[...])
    lane = lax.broadcasted_iota(jnp.int32, logits.shape, 1)
    logits = jnp.where(lane < N_EXPERTS, logits, NEG)
    e = jnp.exp(logits - jnp.max(logits, axis=-1, keepdims=True))
    aff_ref[...] = e / jnp.sum(e, axis=-1, keepdims=True)


def _outproj(gla, na, lru, x, wo, gate, g2, shift, scale, wr):
    T, D = x.shape
    tm = min(512, T)
    rows = lambda w: pl.BlockSpec((tm, w), lambda i: (i, 0))
    full = lambda a: pl.BlockSpec(a.shape, lambda i: (0, 0), pipeline_mode=pl.Buffered(1))
    vec = pl.BlockSpec((1, D), lambda i: (0, 0))
    return pl.pallas_call(
        _outproj_kernel,
        grid=(T // tm,),
        in_specs=[rows(GLA_V), rows(NA_W), rows(LRU_WIDTH), rows(D), full(wo), vec, vec, vec, vec, full(wr)],
        out_specs=[rows(D), rows(LANES)],
        out_shape=[jax.ShapeDtypeStruct((T, D), F32), jax.ShapeDtypeStruct((T, LANES), F32)],
        scratch_shapes=[pltpu.VMEM((tm, wo.shape[0]), BF16)],
        compiler_params=_params("arbitrary"),
    )(gla, na, lru, x, wo, gate, g2, shift, scale, wr)


def _gather_kernel(idx_ref, h_hbm, g2_ref, sh_ref, sc_ref, o_ref, buf, sem, *, rt, nt, ne):
    e = pl.program_id(0)
    t = pl.program_id(1)

    step = e * nt + t
    slot = step % 2

    def start_rows(first, s):
        def issue(r, carry):
            tok = idx_ref[first + r]
            pltpu.make_async_copy(h_hbm.at[pl.ds(tok, 1)], buf.at[s, pl.ds(r, 1)], sem.at[s]).start()
            return carry

        lax.fori_loop(0, rt, issue, 0, unroll=ISSUE_UNROLL)

    @pl.when(step == 0)
    def _():
        start_rows(0, 0)

    @pl.when(step + 1 < ne * nt)
    def _():
        start_rows((step + 1) * rt, 1 - slot)

    pltpu.make_async_copy(h_hbm.at[pl.ds(0, rt)], buf.at[slot], sem.at[slot]).wait()
    o_ref[0] = _norm2(buf[slot], g2_ref, sh_ref, sc_ref).astype(BF16)


def _gather_rows(x, idx, g2, shift, scale):
    T, D = x.shape
    E, cap = idx.shape
    rt = min(ROW_TILE, cap)
    nt = cap // rt
    vec = pl.BlockSpec((1, D), lambda e, t, idx: (0, 0))
    return pl.pallas_call(
        functools.partial(_gather_kernel, rt=rt, nt=nt, ne=E),
        grid_spec=pltpu.PrefetchScalarGridSpec(
            num_scalar_prefetch=1,
            grid=(E, nt),
            in_specs=[pl.BlockSpec(memory_space=pl.ANY), vec, vec, vec],
            out_specs=pl.BlockSpec((1, rt, D), lambda e, t, idx: (e, t, 0)),
            scratch_shapes=[pltpu.VMEM((2, rt, D), F32), pltpu.SemaphoreType.DMA((2,))]),
        out_shape=jax.ShapeDtypeStruct((E, cap, D), BF16),
        compiler_params=_params("arbitrary", "arbitrary"),
    )(idx.reshape(-1), x, g2, shift, scale)


def _combine_kernel(idx_ref, ye_ref, x_in, x_out, buf, gsem, ssem, *, rt, nt):
    del x_in
    e = pl.program_id(0)
    t = pl.program_id(1)
    slot = t % 2

    def start_rows(tile, s, gather):
        first = (e * nt + tile) * rt

        def body(r, carry):
            tok = idx_ref[first + r]
            hbm_row = x_out.at[pl.ds(tok, 1)]
            vmem_row = buf.at[s, pl.ds(r, 1)]
            if gather:
                pltpu.make_async_copy(hbm_row, vmem_row, gsem.at[s]).start()
            else:
                pltpu.make_async_copy(vmem_row, hbm_row, ssem.at[s]).start()
            return carry

        lax.fori_loop(0, rt, body, 0, unroll=ISSUE_UNROLL)

    def wait_rows(s, gather):
        if gather:
            pltpu.make_async_copy(x_out.at[pl.ds(0, rt)], buf.at[s], gsem.at[s]).wait()
        else:
            pltpu.make_async_copy(buf.at[s], x_out.at[pl.ds(0, rt)], ssem.at[s]).wait()

    @pl.when(t == 0)
    def _():
        start_rows(0, 0, True)

    wait_rows(slot, True)

    @pl.when(t + 1 < nt)
    def _():
        @pl.when(t >= 1)
        def _():
            wait_rows(1 - slot, False)

        start_rows(t + 1, 1 - slot, True)

    buf[slot] = buf[slot] + ye_ref[0]
    start_rows(t, slot, False)

    @pl.when(t == nt - 1)
    def _():
        wait_rows(slot, False)
        if nt > 1:
            wait_rows(1 - slot, False)


def _combine(x, ye, idx):
    E, cap, D = ye.shape
    rt = min(ROW_TILE, cap)
    nt = cap // rt
    return pl.pallas_call(
        functools.partial(_combine_kernel, rt=rt, nt=nt),
        grid_spec=pltpu.PrefetchScalarGridSpec(
            num_scalar_prefetch=1,
            grid=(E, nt),
            in_specs=[pl.BlockSpec((1, rt, D), lambda e, t, idx: (e, t, 0)),
                      pl.BlockSpec(memory_space=pl.ANY)],
            out_specs=pl.BlockSpec(memory_space=pl.ANY),
            scratch_shapes=[pltpu.VMEM((2, rt, D), F32), pltpu.SemaphoreType.DMA((2,)),
                            pltpu.SemaphoreType.DMA((2,))]),
        out_shape=jax.ShapeDtypeStruct(x.shape, F32),
        input_output_aliases={2: 0},
        compiler_params=_params("arbitrary", "arbitrary"),
    )(idx.reshape(-1), ye, x)


def _moe_kernel(*refs, nj, has_ctx):
    it = iter(refs)
    xs_ref, tw_ref, gate_ref = next(it), next(it), next(it)
    xc_ref, twc_ref, gatec_ref = (next(it), next(it), next(it)) if has_ctx else (None, None, None)
    wg_ref, wu_ref, wd_ref = next(it), next(it), next(it)
    ye_ref = next(it)
    yc_ref = next(it) if has_ctx else None
    hid_scr = next(it)
    hidc_scr = next(it) if has_ctx else None
    s = pl.program_id(1)
    tf = wg_ref.shape[3]

    def row_chunks(n_rows):
        step = min(MOE_ROW_CHUNK, n_rows)
        return [slice(m * step, (m + 1) * step) for m in range(n_rows // step)]

    def up(x_ref, h_scr):
        wg = wg_ref[0, 0].astype(BF16)
        wu = wu_ref[0, 0].astype(BF16)
        for rs in row_chunks(x_ref.shape[1]):
            x = x_ref[0, rs]
            a = _dot(x, wg)
            h_scr[s, rs] = ((a * jax.nn.sigmoid(a)) * _dot(x, wu)).astype(BF16)

    def down(h_scr, y_ref, w_ref, g_ref):
        wd = wd_ref[0, 0].astype(BF16)
        for rs in row_chunks(y_ref.shape[1]):
            acc = _dot(h_scr[0, rs], wd[0:tf])
            for j in range(1, nj):
                acc = acc + _dot(h_scr[j, rs], wd[j * tf:(j + 1) * tf])
            y_ref[0, rs] = acc * w_ref[0, rs] * g_ref[...]

    @pl.when(s < nj)
    def _():
        up(xs_ref, hid_scr)
        if has_ctx:
            up(xc_ref, hidc_scr)

    @pl.when(s >= nj)
    def _():
        down(hid_scr, ye_ref, tw_ref, gate_ref)
        if has_ctx:
            down(hidc_scr, yc_ref, twc_ref, gatec_ref)


def _moe(xs, tw, gate, layer, w_gate, w_up, w_down, ctx=None):
    E, cap, D = xs.shape
    FF = w_gate.shape[3]
    tf = tn = MOE_TILE
    nj, nn = FF // tf, D // tn
    has_ctx = ctx is not None
    jj = lambda s: jnp.minimum(s, nj - 1)
    nc = lambda s: jnp.maximum(s - nj, 0)
    gvec = pl.BlockSpec((1, tn), lambda e, s: (0, nc(s)))
    args = [xs, tw, gate]
    specs = [pl.BlockSpec((1, cap, D), lambda e, s: (e, 0, 0)),
             pl.BlockSpec((1, cap, 1), lambda e, s: (e, 0, 0)), gvec]
    outs = [pl.BlockSpec((1, cap, tn), lambda e, s: (e, 0, nc(s)))]
    shapes = [jax.ShapeDtypeStruct((E, cap, D), F32)]
    scratch = [pltpu.VMEM((nj, cap, tf), BF16)]
    if has_ctx:
        xc, twc, gatec = ctx
        cc = xc.shape[1]
        args += [xc, twc, gatec]
        specs += [pl.BlockSpec((1, cc, D), lambda e, s: (e, 0, 0)),
                  pl.BlockSpec((1, cc, 1), lambda e, s: (e, 0, 0)), gvec]
        outs.append(pl.BlockSpec((1, cc, tn), lambda e, s: (e, 0, nc(s))))
        shapes.append(jax.ShapeDtypeStruct((E, cc, D), F32))
        scratch.append(pltpu.VMEM((nj, cc, tf), BF16))
    args += [w_gate, w_up, w_down]
    specs += [pl.BlockSpec((1, 1, D, tf), lambda e, s: (layer, e, 0, jj(s))),
              pl.BlockSpec((1, 1, D, tf), lambda e, s: (layer, e, 0, jj(s))),
              pl.BlockSpec((1, 1, FF, tn), lambda e, s: (layer, e, 0, nc(s)))]
    res = pl.pallas_call(
        functools.partial(_moe_kernel, nj=nj, has_ctx=has_ctx),
        grid=(E, nj + nn),
        in_specs=specs,
        out_specs=outs,
        out_shape=shapes,
        scratch_shapes=scratch,
        compiler_params=_params("arbitrary", "arbitrary"),
    )(*args)
    return res if has_ctx else (res[0], None)


def _select_kernel(a_ref, pos_ref, *, cap):
    a = a_ref[0]
    R = a.shape[0]

    def count(mask):
        return jnp.sum(jnp.where(mask, 1.0, 0.0))

    def bisect(_, bounds):
        lo, hi = bounds
        mid = 0.5 * (lo + hi)
        enough = count(a >= mid) >= cap
        return jnp.where(enough, mid, lo), jnp.where(enough, hi, mid)

    lo, hi = lax.fori_loop(0, SELECT_BISECTIONS, bisect, (jnp.float32(0.0), jnp.float32(2.0)))
    above = a >= hi
    equal = jnp.logical_and(a >= lo, a < hi)
    need = cap - count(above)

    li = lax.broadcasted_iota(jnp.int32, (LANES, LANES), 0)
    lj = lax.broadcasted_iota(jnp.int32, (LANES, LANES), 1)
    upper = jnp.where(li <= lj, 1.0, 0.0).astype(BF16)
    ones = jnp.ones((LANES, LANES), BF16)
    ri = lax.broadcasted_iota(jnp.int32, (R, R), 0)
    rj = lax.broadcasted_iota(jnp.int32, (R, R), 1)
    earlier_rows = jnp.where(rj < ri, 1.0, 0.0).astype(BF16)

    def rank(mask):
        x = jnp.where(mask, 1.0, 0.0)
        xb = x.astype(BF16)
        return _dot(xb, upper) - x + _dot(earlier_rows, _dot(xb, ones).astype(BF16))

    take = jnp.logical_or(above, jnp.logical_and(equal, rank(equal) < need))
    pos_ref[0] = jnp.where(take, rank(take).astype(jnp.int32), -1)


def _select(aff, cap):
    T = aff.shape[0]
    R = pl.cdiv(pl.cdiv(T, LANES), LANES) * LANES
    a = jnp.full((N_EXPERTS, R * LANES), -1.0, F32).at[:, :T].set(aff[:, :N_EXPERTS].T)
    spec = pl.BlockSpec((1, R, LANES), lambda e: (e, 0, 0))
    pos = pl.pallas_call(
        functools.partial(_select_kernel, cap=cap),
        grid=(N_EXPERTS,),
        in_specs=[spec],
        out_specs=spec,
        out_shape=jax.ShapeDtypeStruct((N_EXPERTS, R, LANES), jnp.int32),
        compiler_params=_params("arbitrary"),
    )(a.reshape(N_EXPERTS, R, LANES))
    return pos.reshape(N_EXPERTS, R * LANES)[:, :T]


def _route(aff, x, g2, shift, scale):
    T = aff.shape[0]
    cap = CAPACITY * T // N_EXPERTS
    pos = _select(aff, cap)
    tok = jnp.broadcast_to(jnp.arange(T, dtype=jnp.int32)[None], pos.shape)
    top_idx = jnp.zeros((N_EXPERTS, cap + 1), jnp.int32).at[
        jnp.arange(N_EXPERTS)[:, None], jnp.where(pos < 0, cap, pos)].set(tok)[:, :cap]
    top_w = jnp.take_along_axis(aff[:, :N_EXPERTS].T, top_idx, axis=1)
    return _gather_rows(x, top_idx, g2, shift, scale), top_w[..., None], top_idx


def _reorder_w_in(w):
    D = w.shape[0]
    sizes = (GLA_QK, GLA_QK, GLA_V, GLA_V, GLA_GATE_RANK, GLA_GATE_RANK, NA_W, NA_W, NA_W, LRU_WIDTH, LRU_WIDTH)
    parts, o = [], 0
    for s in sizes:
        parts.append(w[:, o:o + s])
        o += s
    gq, gk, gv, gr, lf, lb, nq, nk, nv, lx, ly = parts
    pad = jnp.zeros((D, LANES - 2 * GLA_GATE_RANK), w.dtype)
    main = jnp.concatenate([nq, nk, nv, gv, gr, lx, ly, gq, gk], axis=1).astype(BF16)
    return main, jnp.concatenate([lf, lb, pad], axis=1).astype(BF16)


def kernel(x, c, ctx, c_ctx, norm1_g, norm2_g, w_mod, b_mod, w_in, gla_w_gate, gla_b_gate, gla_norm_g, na_q_norm_g, na_k_norm_g, na_rpb, lru_conv_w, lru_conv_b, lru_w_a, lru_b_a, lru_w_i, lru_b_i, lru_lambda, w_out, w_router, w_exp_gate, w_exp_up, w_exp_down):
    B, T, D = x.shape
    Lc = ctx.shape[1]
    depth = w_in.shape[0]
    assert B == 1 and T % (NA_ROWS_PER_STEP * GRID_W) == 0 and T // GRID_W >= NA_UNION_ROWS
    xs = x[0]
    cs = ctx[0]
    mod = _modulation(jnp.stack([c[0], c_ctx]), w_mod, b_mod)
    cos, sin = _rope_tables(T)
    tb_l, tb_c = min(512, T), min(512, Lc)
    tri = {(tb, rev): _tri_matrix(tb, rev) for tb in {tb_l, tb_c} for rev in (False, True)}
    row = lambda v: v.reshape(1, -1)

    for l in range(depth):
        ctx_out = l < depth - 1
        mx = [row(m) for m in jnp.split(mod[l, 0], 6)]
        mc = [row(m) for m in jnp.split(mod[l, 1], 6)]
        w_main, w_lr = _reorder_w_in(w_in[l])
        g1 = row(norm1_g[l])
        zx, zx_lr = _inproj(xs, g1, mx[0], mx[1], w_main, w_lr)
        zc, zc_lr = _inproj(cs, g1, mc[0], mc[1], w_main, w_lr)

        gla_dir = []
        for d in range(2):
            wg = jnp.zeros((LANES, GLA_QK), F32).at[d * GLA_GATE_RANK:(d + 1) * GLA_GATE_RANK].set(gla_w_gate[l, d])
            gla_dir.append((wg, row(gla_b_gate[l, d])))
        ng = row(gla_norm_g[l])
        zero_state = jnp.zeros((GLA_HEADS, GLA_DV, GLA_DK), F32)
        oc_f, sc_f = _gla_pass(zc, zc_lr, None, None, *gla_dir[0], tri[(tb_c, False)], zero_state, reverse=False)
        gla_c, sc_b = _gla_pass(zc, zc_lr, None, None, *gla_dir[1], tri[(tb_c, True)], zero_state, reverse=True,
                                oprev=oc_f, norm_g=ng)
        ol_f, _ = _gla_pass(zx, zx_lr, cos, sin, *gla_dir[0], tri[(tb_l, False)], sc_f, reverse=False)
        gla_x, _ = _gla_pass(zx, zx_lr, cos, sin, *gla_dir[1], tri[(tb_l, True)], sc_b, reverse=True,
                             oprev=ol_f, norm_g=ng)

        qg, kg = row(na_q_norm_g[l]), row(na_k_norm_g[l])
        ql, kl, vl = _na_prep(zx, qg, kg)
        qc, kc, vc = _na_prep(zc, qg, kg)
        na_x = _na_latent(ql, kl, vl, kc, vc, _na_bias(na_rpb[l]))

        lru_dir = []
        for d in range(2):
            lru_dir.append(dict(conv_w=lru_conv_w[l], conv_b=row(lru_conv_b[l]),
                                w_a=lru_w_a[l, d].astype(BF16), b_a=row(lru_b_a[l, d]),
                                w_i=lru_w_i[l, d].astype(BF16), b_i=row(lru_b_i[l, d]),
                                lam=row(lru_lambda[l, d])))
        zero_h = jnp.zeros((1, LRU_WIDTH), F32)
        hc_f, fin_f = _lru_pass(zc, lru_dir[0], zero_h, reverse=False)
        lru_c, fin_b = _lru_pass(zc, lru_dir[1], zero_h, reverse=True, hprev=hc_f)
        hl_f, _ = _lru_pass(zx, lru_dir[0], fin_f[0:1], reverse=False)
        lru_x, _ = _lru_pass(zx, lru_dir[1], fin_b[0:1], reverse=True, hprev=hl_f)

        wo = w_out[l].astype(BF16)
        wr = jnp.zeros((D, LANES), BF16).at[:, :N_EXPERTS].set(w_router[l].astype(BF16))
        g2 = row(norm2_g[l])
        x1, affx = _outproj(gla_x, na_x, lru_x, xs, wo, mx[2], g2, mx[3], mx[4], wr)
        xg, twx, idxx = _route(affx, x1, g2, mx[3], mx[4])
        if ctx_out:
            na_c = _na_context(qc, kc, vc)
            c1, affc = _outproj(gla_c, na_c, lru_c, cs, wo, mc[2], g2, mc[3], mc[4], wr)
            cg, twc, idxc = _route(affc, c1, g2, mc[3], mc[4])
            ye, yc = _moe(xg, twx, mx[5], l, w_exp_gate, w_exp_up, w_exp_down, ctx=(cg, twc, mc[5]))
            cs = _combine(c1, yc, idxc)
        else:
            ye, _ = _moe(xg, twx, mx[5], l, w_exp_gate, w_exp_up, w_exp_down)
        xs = _combine(x1, ye, idxx)
    return xs[None]
```

```python
import functools

import numpy as np
import jax
import jax.numpy as jnp
from jax import lax
from jax.experimental import pallas as pl
from jax.experimental.pallas import tpu as pltpu

F32 = jnp.float32
BF16 = jnp.bfloat16

GRID_W = 64
EPS = 1e-6
GLA_HEADS = 4
GLA_DK = 64
GLA_DV = 128
GLA_GATE_RANK = 16
GLA_TAU = 16.0
GLA_CHUNK = 64
ROPE_BASE = 10000.0
NA_HEADS = 8
NA_HD = 128
NA_KH = 8
NA_KW = 16
LRU_WIDTH = 512
LRU_BLOCKS = 4
LRU_CONV = 4
LRU_C = 8.0
N_EXPERTS = 16
CAPACITY = 2

GLA_QK = GLA_HEADS * GLA_DK
GLA_V = GLA_HEADS * GLA_DV
NA_W = NA_HEADS * NA_HD

LANES = 128
SUBLANES = 8
VMEM_LIMIT = 56 * 1024 * 1024
NEG = -1e30
ROW_TILE = 256
ISSUE_UNROLL = 8
SELECT_BISECTIONS = 40
MOE_TILE = 256
MOE_ROW_CHUNK = 512
LRU_HALO = 16
NA_ROWS_PER_STEP = 8
NA_UNION_ROWS = 16
NA_Q_CHUNK = 256
NA_PAIR_OFFSET = NA_UNION_ROWS - NA_KH
NA_PAIR_ENTRIES = NA_UNION_ROWS + 2 * NA_KH - 2

Z_NAQ, Z_NAK, Z_NAV = 0, NA_W, 2 * NA_W
Z_GV = 3 * NA_W
Z_GR = Z_GV + GLA_V
Z_LX = Z_GR + GLA_V
Z_LY = Z_LX + LRU_WIDTH
Z_GQ = Z_LY + LRU_WIDTH
Z_GK = Z_GQ + GLA_QK
Z_COLS = Z_GK + GLA_QK
Z_CHUNK = 512


def _params(*sem):
    return pltpu.CompilerParams(dimension_semantics=sem, vmem_limit_bytes=VMEM_LIMIT)


def _dot(a, b):
    return jnp.dot(a, b, preferred_element_type=F32)


def _dot_nt(a, b):
    return lax.dot_general(a, b, (((1,), (1,)), ((), ())), preferred_element_type=F32)


def _dot_tn(a, b):
    return lax.dot_general(a, b, (((0,), (0,)), ((), ())), preferred_element_type=F32)


def _split2(a):
    hi = a.astype(BF16)
    lo = (a - hi.astype(F32)).astype(BF16)
    return hi, lo


def _dot_hi(a, b):
    ah, al = _split2(a)
    bh, bl = _split2(b)
    return _dot(ah, bh) + _dot(al, bh) + _dot(ah, bl)


def _rms(x):
    return x * lax.rsqrt(jnp.mean(x * x, axis=-1, keepdims=True) + EPS)


def _norm2(x, g_ref, sh_ref, sc_ref):
    return (_rms(x) * g_ref[...]) * (1.0 + sc_ref[...]) + sh_ref[...]


def _log_sigmoid(z):
    return jnp.minimum(z, 0.0) - jnp.log1p(jnp.exp(-jnp.abs(z)))


def _mod_kernel(cb_ref, w_ref, b_ref, o_ref):
    tn = w_ref.shape[2]
    s = [cb_ref[v] * jax.nn.sigmoid(cb_ref[v]) for v in range(2)]
    for j in range(tn // LANES):
        cs = slice(j * LANES, (j + 1) * LANES)
        wj = w_ref[0, :, cs]
        for v in range(2):
            o_ref[0, v:v + 1, cs] = jnp.sum(wj * s[v], axis=0, keepdims=True) + b_ref[0, :, cs]


def _modulation(c2, w_mod, b_mod):
    L, D, N6 = w_mod.shape
    tn = 1024
    cb = jnp.broadcast_to(c2[:, :, None], (2, D, LANES))
    return pl.pallas_call(
        _mod_kernel,
        grid=(L, N6 // tn),
        in_specs=[pl.BlockSpec((2, D, LANES), lambda l, j: (0, 0, 0)),
                  pl.BlockSpec((1, D, tn), lambda l, j: (l, 0, j)),
                  pl.BlockSpec((1, 1, tn), lambda l, j: (l, 0, j))],
        out_specs=pl.BlockSpec((1, 2, tn), lambda l, j: (l, 0, j)),
        out_shape=jax.ShapeDtypeStruct((L, 2, N6), F32),
        compiler_params=_params("arbitrary", "arbitrary"),
    )(cb, w_mod, b_mod.reshape(L, 1, N6))


def _inproj_kernel(x_ref, g_ref, sh_ref, sc_ref, w_ref, wlr_ref, o_ref, olr_ref):
    h = _norm2(x_ref[...], g_ref, sh_ref, sc_ref).astype(BF16)
    for n in range(w_ref.shape[1] // Z_CHUNK):
        cs = slice(n * Z_CHUNK, (n + 1) * Z_CHUNK)
        o_ref[:, cs] = _dot(h, w_ref[:, cs]).astype(BF16)
    olr_ref[...] = _dot(h, wlr_ref[...])


def _inproj(x, gain, shift, scale, w, wlr):
    T, D = x.shape
    NZ = w.shape[1]
    tm = min(512, T)
    vec = pl.BlockSpec((1, D), lambda i: (0, 0))
    resident = lambda a: pl.BlockSpec(a.shape, lambda i: (0, 0), pipeline_mode=pl.Buffered(1))
    return pl.pallas_call(
        _inproj_kernel,
        grid=(T // tm,),
        in_specs=[pl.BlockSpec((tm, D), lambda i: (i, 0)), vec, vec, vec, resident(w), resident(wlr)],
        out_specs=[pl.BlockSpec((tm, NZ), lambda i: (i, 0)), pl.BlockSpec((tm, LANES), lambda i: (i, 0))],
        out_shape=[jax.ShapeDtypeStruct((T, NZ), BF16), jax.ShapeDtypeStruct((T, LANES), F32)],
        compiler_params=_params("arbitrary"),
    )(x, gain, shift, scale, w, wlr)


def _gla_kernel(*refs, reverse, rope, final, tb):
    it = iter(refs)
    q_ref, k_ref, v_ref, lr_ref = next(it), next(it), next(it), next(it)
    cos_ref, sin_ref = (next(it), next(it)) if rope else (None, None)
    wg_ref, bg_ref, tri_ref, s0_ref = next(it), next(it), next(it), next(it)
    oprev_ref, r_ref, ng_ref = (next(it), next(it), next(it)) if final else (None, None, None)
    o_ref, sfin_ref, st_scr = next(it), next(it), next(it)

    @pl.when(pl.program_id(0) == 0)
    def _():
        st_scr[...] = s0_ref[...]

    q = q_ref[...].astype(F32)
    k = k_ref[...].astype(F32)
    if rope:
        lane = lax.broadcasted_iota(jnp.int32, q.shape, 1)
        first = (lane % GLA_DK) < (GLA_DK // 2)
        cos, sin = cos_ref[...], sin_ref[...]

        def rot(t):
            return jnp.where(first, pltpu.roll(t, GLA_QK - GLA_DK // 2, 1), pltpu.roll(t, GLA_DK // 2, 1))

        q = q * cos + rot(q) * sin
        k = k * cos + rot(k) * sin
    q = q * (GLA_DK ** -0.5)

    g = _log_sigmoid(_dot_hi(lr_ref[...], wg_ref[...]) + bg_ref[...]) * (1.0 / GLA_TAU)
    g_hi = g.astype(BF16)
    g_r1 = g - g_hi.astype(F32)
    g_mid = g_r1.astype(BF16)
    g_lo = (g_r1 - g_mid.astype(F32)).astype(BF16)
    tri = tri_ref[...]
    gc = _dot(tri, g_hi) + _dot(tri, g_mid) + _dot(tri, g_lo)

    C = GLA_CHUNK
    ri = lax.broadcasted_iota(jnp.int32, (C, C), 0)
    ci = lax.broadcasted_iota(jnp.int32, (C, C), 1)
    causal = (ci >= ri) if reverse else (ci <= ri)
    nchunk = tb // C
    for c in (range(nchunk - 1, -1, -1) if reverse else range(nchunk)):
        sl = slice(c * C, (c + 1) * C)
        gcc = gc[sl]
        gt = gcc[0:1] if reverse else gcc[C - 1:C]
        qe = q[sl] * jnp.exp(gcc)
        ke = k[sl] * jnp.exp(-gcc)
        kd = k[sl] * jnp.exp(gt - gcc)
        dec = jnp.exp(gt)
        for h in range(GLA_HEADS):
            hs = slice(h * GLA_DK, (h + 1) * GLA_DK)
            vs = slice(h * GLA_DV, (h + 1) * GLA_DV)
            qh = qe[:, hs].astype(BF16)
            vh = v_ref[sl, vs]
            att = jnp.where(causal, _dot_nt(qh, ke[:, hs].astype(BF16)), 0.0)
            st = st_scr[h]
            o = _dot(att.astype(BF16), vh) + _dot_nt(qh, st.astype(BF16))
            st_scr[h] = st * dec[:, hs] + _dot_tn(vh, kd[:, hs].astype(BF16))
            if final:
                y = _rms(oprev_ref[sl, vs] + o) * ng_ref[...]
                rg = r_ref[sl, vs].astype(F32)
                o_ref[sl, vs] = (y * (rg * jax.nn.sigmoid(rg))).astype(o_ref.dtype)
            else:
                o_ref[sl, vs] = o
    sfin_ref[...] = st_scr[...]


def _gla_pass(z, zlr, cos, sin, wg, bg, tri, s0, *, reverse, oprev=None, norm_g=None):
    T = z.shape[0]
    tb = tri.shape[0]
    nb = T // tb
    rope = cos is not None
    final = oprev is not None
    blk = (lambda i: nb - 1 - i) if reverse else (lambda i: i)

    def col(width, start):
        return pl.BlockSpec((tb, width), lambda i: (blk(i), start // width))

    const2 = lambda shape: pl.BlockSpec(shape, lambda i: (0, 0))
    state = pl.BlockSpec((GLA_HEADS, GLA_DV, GLA_DK), lambda i: (0, 0, 0))
    args = [z, z, z, zlr]
    specs = [col(GLA_QK, Z_GQ), col(GLA_QK, Z_GK), col(GLA_V, Z_GV), col(LANES, 0)]
    if rope:
        args += [cos, sin]
        specs += [col(GLA_QK, 0), col(GLA_QK, 0)]
    args += [wg, bg, tri, s0]
    specs += [const2((LANES, GLA_QK)), const2((1, GLA_QK)), const2((tb, tb)), state]
    if final:
        args += [oprev, z, norm_g]
        specs += [col(GLA_V, 0), col(GLA_V, Z_GR), const2((1, GLA_DV))]
    return pl.pallas_call(
        functools.partial(_gla_kernel, reverse=reverse, rope=rope, final=final, tb=tb),
        grid=(nb,),
        in_specs=specs,
        out_specs=[col(GLA_V, 0), state],
        out_shape=[jax.ShapeDtypeStruct((T, GLA_V), BF16 if final else F32),
                   jax.ShapeDtypeStruct((GLA_HEADS, GLA_DV, GLA_DK), F32)],
        scratch_shapes=[pltpu.VMEM((GLA_HEADS, GLA_DV, GLA_DK), F32)],
        compiler_params=_params("arbitrary"),
    )(*args)


def _tri_matrix(tb, reverse):
    i = np.arange(tb)
    same = (i[:, None] // GLA_CHUNK) == (i[None, :] // GLA_CHUNK)
    order = (i[None, :] >= i[:, None]) if reverse else (i[None, :] <= i[:, None])
    return jnp.asarray(same & order, BF16)


def _rope_tables(T):
    pos = jnp.arange(T)
    row = (pos // GRID_W).astype(F32)
    col = (pos % GRID_W).astype(F32)
    nf = GLA_DK // 4
    inv = ROPE_BASE ** (-jnp.arange(nf, dtype=F32) / nf)
    ang = jnp.concatenate([row[:, None] * inv, col[:, None] * inv], axis=-1)
    cos, sin = jnp.cos(ang), jnp.sin(ang)
    cos_full = jnp.tile(jnp.concatenate([cos, cos], axis=-1), (1, GLA_HEADS))
    sin_full = jnp.tile(jnp.concatenate([-sin, sin], axis=-1), (1, GLA_HEADS))
    return cos_full, sin_full


def _naprep_kernel(q_ref, k_ref, v_ref, qg_ref, kg_ref, qo_ref, ko_ref, vo_ref):
    for h in range(NA_HEADS):
        hs = slice(h * NA_HD, (h + 1) * NA_HD)
        qo_ref[h] = (_rms(q_ref[:, hs].astype(F32)) * qg_ref[...] * (NA_HD ** -0.5)).astype(BF16)
        ko_ref[h] = (_rms(k_ref[:, hs].astype(F32)) * kg_ref[...]).astype(BF16)
        vo_ref[h] = v_ref[:, hs]


def _na_prep(z, qg, kg):
    T = z.shape[0]
    tm = min(512, T)
    col = lambda c: pl.BlockSpec((tm, NA_W), lambda i: (i, c))
    vec = pl.BlockSpec((1, NA_HD), lambda i: (0, 0))
    out = pl.BlockSpec((NA_HEADS, tm, NA_HD), lambda i: (0, i, 0))
    shp = jax.ShapeDtypeStruct((NA_HEADS, T, NA_HD), BF16)
    return pl.pallas_call(
        _naprep_kernel,
        grid=(T // tm,),
        in_specs=[col(Z_NAQ // NA_W), col(Z_NAK // NA_W), col(Z_NAV // NA_W), vec, vec],
        out_specs=[out, out, out],
        out_shape=[shp, shp, shp],
        compiler_params=_params("arbitrary"),
    )(z, z, z, qg, kg)


def _na_kernel(q_ref, k_ref, v_ref, kc_ref, vc_ref, bias_ref, o_ref, *, n_rows):
    i = pl.program_id(1)
    W = GRID_W
    R = NA_ROWS_PER_STEP
    first_row = jnp.clip(i * R - NA_KH // 2, 0, n_rows - NA_UNION_ROWS)
    start = pl.multiple_of(first_row * W, W)
    kw = k_ref[0, pl.ds(start, NA_UNION_ROWS * W), :]
    vw = v_ref[0, pl.ds(start, NA_UNION_ROWS * W), :]
    kc = kc_ref[0]
    vc = vc_ref[0]
    upper = lax.broadcasted_iota(jnp.int32, (1, 2 * W), 1) // W

    def bias_rows(rr):
        r = i * R + rr
        lo = jnp.clip(r - NA_KH // 2, 0, n_rows - NA_KH) - r + (NA_KH - 1)
        blocks = []
        for m in range(NA_UNION_ROWS // 2):
            a_even = first_row + 2 * m - r + (NA_KH - 1)
            a = a_even + upper
            ok = jnp.logical_and(a >= lo, a < lo + NA_KH)
            blocks.append(jnp.where(ok, bias_ref[0, a_even + NA_PAIR_OFFSET], NEG))
        return jnp.concatenate(blocks, axis=1)

    nq = q_ref.shape[1]
    step = min(NA_Q_CHUNK, nq)
    for c in range(nq // step):
        qs = slice(c * step, (c + 1) * step)
        q = q_ref[0, qs, :]
        bias = jnp.concatenate([bias_rows(rr) for rr in range(c * step // W, (c + 1) * step // W)], axis=0)
        s_loc = _dot_nt(q, kw) + bias
        s_ctx = _dot_nt(q, kc)
        m = jnp.maximum(jnp.max(s_loc, axis=-1, keepdims=True), jnp.max(s_ctx, axis=-1, keepdims=True))
        p_loc = jnp.exp(s_loc - m)
        p_ctx = jnp.exp(s_ctx - m)
        den = jnp.sum(p_loc, axis=-1, keepdims=True) + jnp.sum(p_ctx, axis=-1, keepdims=True)
        o = _dot(p_loc.astype(BF16), vw) + _dot(p_ctx.astype(BF16), vc)
        o_ref[qs, :] = (o / den).astype(BF16)


def _na_latent(q, k, v, kc, vc, bias):
    H, T, hd = q.shape
    Lc = kc.shape[1]
    n_rows = T // GRID_W
    R = NA_ROWS_PER_STEP
    nblk = n_rows // R
    whole = lambda n: pl.BlockSpec((1, n, hd), lambda h, i: (h, 0, 0))
    return pl.pallas_call(
        functools.partial(_na_kernel, n_rows=n_rows),
        grid=(H, nblk),
        in_specs=[pl.BlockSpec((1, R * GRID_W, hd), lambda h, i: (h, i, 0)),
                  whole(T), whole(T), whole(Lc), whole(Lc),
                  pl.BlockSpec((1,) + bias.shape[1:], lambda h, i: (h, 0, 0, 0))],
        out_specs=pl.BlockSpec((R * GRID_W, hd), lambda h, i: (i, h)),
        out_shape=jax.ShapeDtypeStruct((T, H * hd), BF16),
        compiler_params=_params("arbitrary", "arbitrary"),
    )(q, k, v, kc, vc, bias)


def _ctxattn_kernel(q_ref, k_ref, v_ref, o_ref):
    s = _dot_nt(q_ref[0], k_ref[0])
    p = jnp.exp(s - jnp.max(s, axis=-1, keepdims=True))
    o = _dot(p.astype(BF16), v_ref[0]) / jnp.sum(p, axis=-1, keepdims=True)
    o_ref[...] = o.astype(BF16)


def _na_context(q, k, v):
    H, Lc, hd = q.shape
    spec = pl.BlockSpec((1, Lc, hd), lambda h: (h, 0, 0))
    return pl.pallas_call(
        _ctxattn_kernel,
        grid=(H,),
        in_specs=[spec, spec, spec],
        out_specs=pl.BlockSpec((Lc, hd), lambda h: (0, h)),
        out_shape=jax.ShapeDtypeStruct((Lc, H * hd), BF16),
        compiler_params=_params("arbitrary"),
    )(q, k, v)


def _na_bias(rpb):
    W = GRID_W
    cols = np.arange(W)
    col_start = np.clip(cols - NA_KW // 2, 0, W - NA_KW)
    col_ok = (cols[None, :] >= col_start[:, None]) & (cols[None, :] < col_start[:, None] + NA_KW)
    dc = np.clip(cols[None, :] - cols[:, None] + (NA_KW - 1), 0, 2 * NA_KW - 2)
    tab = jnp.where(col_ok[None, None], rpb[:, :, dc], NEG).astype(F32)
    n_tab = 2 * NA_KH - 1
    pad_lo = jnp.full((rpb.shape[0], NA_PAIR_OFFSET, W, W), NEG, F32)
    pad_hi = jnp.full((rpb.shape[0], NA_PAIR_ENTRIES + 1 - NA_PAIR_OFFSET - n_tab, W, W), NEG, F32)
    ext = jnp.concatenate([pad_lo, tab, pad_hi], axis=1)
    return jnp.concatenate([ext[:, :-1], ext[:, 1:]], axis=-1)


def _lru_kernel(*refs, reverse, final, tb, nblk):
    it = iter(refs)
    x_ref, xp_ref, xn_ref = next(it), next(it), next(it)
    cw_ref, cb_ref, wa_ref, ba_ref, wi_ref, bi_ref, lam_ref, h0_ref = (next(it) for _ in range(8))
    hprev_ref, y_ref = (next(it), next(it)) if final else (None, None)
    o_ref, hfin_ref = next(it), next(it)
    xx_scr, a_scr, b_scr, carry_scr = next(it), next(it), next(it), next(it)

    i = pl.program_id(0)
    blk = (nblk - 1 - i) if reverse else i
    S = SUBLANES
    Wd = LRU_WIDTH

    @pl.when(i == 0)
    def _():
        carry_scr[...] = jnp.broadcast_to(h0_ref[...], (S, Wd))

    H = LRU_HALO
    xx_scr[0:H] = jnp.where(blk > 0, xp_ref[...].astype(F32), 0.0)
    xx_scr[H:H + tb] = x_ref[...].astype(F32)
    xx_scr[H + tb:2 * H + tb] = jnp.where(blk < nblk - 1, xn_ref[...].astype(F32), 0.0)
    xc = cb_ref[...]
    for j in range(LRU_CONV):
        xc = xc + cw_ref[j:j + 1] * xx_scr[H - 1 + j:H - 1 + j + tb]

    bw = Wd // LRU_BLOCKS
    ra, ri = [], []
    for n in range(LRU_BLOCKS):
        xb = xc[:, n * bw:(n + 1) * bw].astype(BF16)
        ra.append(_dot(xb, wa_ref[n]))
        ri.append(_dot(xb, wi_ref[n]))
    rg = jax.nn.sigmoid(jnp.concatenate(ra, axis=1) + ba_ref[...])
    ig = jax.nn.sigmoid(jnp.concatenate(ri, axis=1) + bi_ref[...])
    log_a = LRU_C * rg * _log_sigmoid(lam_ref[...])
    a = jnp.exp(log_a)
    a_scr[...] = a
    b_scr[...] = jnp.sqrt(-jnp.tanh(log_a) * (a * a + 1.0)) * (ig * xc)

    row = lax.broadcasted_iota(jnp.int32, (S, Wd), 0)
    ng = tb // S

    def body(gi, carry):
        g = (ng - 1 - gi) if reverse else gi
        off = pl.multiple_of(g * S, S)
        a = a_scr[pl.ds(off, S), :]
        b = b_scr[pl.ds(off, S), :]
        for s in (1, 2, 4):
            sh = (S - s) if reverse else s
            keep = (row < S - s) if reverse else (row >= s)
            b = jnp.where(keep, a * pltpu.roll(b, sh, 0) + b, b)
            a = jnp.where(keep, a * pltpu.roll(a, sh, 0), a)
        h = b + a * carry
        b_scr[pl.ds(off, S), :] = h
        return jnp.broadcast_to(h[0:1] if reverse else h[S - 1:S], (S, Wd))

    carry = lax.fori_loop(0, ng, body, carry_scr[...])
    carry_scr[...] = carry
    hfin_ref[...] = carry
    if final:
        o_ref[...] = ((hprev_ref[...] + b_scr[...]) * jax.nn.gelu(y_ref[...].astype(F32))).astype(o_ref.dtype)
    else:
        o_ref[...] = b_scr[...]


def _lru_pass(z, p, h0, *, reverse, hprev=None):
    T = z.shape[0]
    tb = min(512, T)
    nb = T // tb
    S = SUBLANES
    Wd = LRU_WIDTH
    final = hprev is not None
    blk = (lambda i: nb - 1 - i) if reverse else (lambda i: i)
    H = LRU_HALO
    gpb = tb // H
    xcol = Z_LX // Wd
    row = pl.BlockSpec((1, Wd), lambda i: (0, 0))
    wspec = pl.BlockSpec((LRU_BLOCKS, Wd // LRU_BLOCKS, Wd // LRU_BLOCKS), lambda i: (0, 0, 0))
    args = [z, z, z, p["conv_w"], p["conv_b"], p["w_a"], p["b_a"], p["w_i"], p["b_i"], p["lam"], h0]
    specs = [pl.BlockSpec((tb, Wd), lambda i: (blk(i), xcol)),
             pl.BlockSpec((H, Wd), lambda i: (jnp.maximum(blk(i) * gpb - 1, 0), xcol)),
             pl.BlockSpec((H, Wd), lambda i: (jnp.minimum((blk(i) + 1) * gpb, T // H - 1), xcol)),
             pl.BlockSpec((LRU_CONV, Wd), lambda i: (0, 0)), row, wspec, row, wspec, row, row, row]
    if final:
        args += [hprev, z]
        specs += [pl.BlockSpec((tb, Wd), lambda i: (blk(i), 0)),
                  pl.BlockSpec((tb, Wd), lambda i: (blk(i), Z_LY // Wd))]
    return pl.pallas_call(
        functools.partial(_lru_kernel, reverse=reverse, final=final, tb=tb, nblk=nb),
        grid=(nb,),
        in_specs=specs,
        out_specs=[pl.BlockSpec((tb, Wd), lambda i: (blk(i), 0)), pl.BlockSpec((S, Wd), lambda i: (0, 0))],
        out_shape=[jax.ShapeDtypeStruct((T, Wd), BF16 if final else F32),
                   jax.ShapeDtypeStruct((S, Wd), F32)],
        scratch_shapes=[pltpu.VMEM((tb + 2 * H, Wd), F32), pltpu.VMEM((tb, Wd), F32),
                        pltpu.VMEM((tb, Wd), F32), pltpu.VMEM((S, Wd), F32)],
        compiler_params=_params("arbitrary"),
    )(*args)


def _outproj_kernel(gla_ref, na_ref, lru_ref, x_ref, wo_ref, gate_ref, g2_ref, sh_ref, sc_ref,
                    wr_ref, xo_ref, aff_ref, mix_scr):
    mix_scr[:, 0:GLA_V] = gla_ref[...]
    mix_scr[:, GLA_V:GLA_V + NA_W] = na_ref[...]
    mix_scr[:, GLA_V + NA_W:] = lru_ref[...]
    xn = x_ref[...] + gate_ref[...] * _dot(mix_scr[...], wo_ref[...])
    xo_ref[...] = xn
    h = _norm2(xn, g2_ref, sh_ref, sc_ref)
    logits = _dot(h.astype(BF16), wr_ref[...])
    lane = lax.broadcasted_iota(jnp.int32, logits.shape, 1)
    logits = jnp.where(lane < N_EXPERTS, logits, NEG)
    e = jnp.exp(logits - jnp.max(logits, axis=-1, keepdims=True))
    aff_ref[...] = e / jnp.sum(e, axis=-1, keepdims=True)


def _outproj(gla, na, lru, x, wo, gate, g2, shift, scale, wr):
    T, D = x.shape
    tm = min(512, T)
    rows = lambda w: pl.BlockSpec((tm, w), lambda i: (i, 0))
    full = lambda a: pl.BlockSpec(a.shape, lambda i: (0, 0), pipeline_mode=pl.Buffered(1))
    vec = pl.BlockSpec((1, D), lambda i: (0, 0))
    return pl.pallas_call(
        _outproj_kernel,
        grid=(T // tm,),
        in_specs=[rows(GLA_V), rows(NA_W), rows(LRU_WIDTH), rows(D), full(wo), vec, vec, vec, vec, full(wr)],
        out_specs=[rows(D), rows(LANES)],
        out_shape=[jax.ShapeDtypeStruct((T, D), F32), jax.ShapeDtypeStruct((T, LANES), F32)],
        scratch_shapes=[pltpu.VMEM((tm, wo.shape[0]), BF16)],
        compiler_params=_params("arbitrary"),
    )(gla, na, lru, x, wo, gate, g2, shift, scale, wr)


def _gather_kernel(idx_ref, h_hbm, g2_ref, sh_ref, sc_ref, o_ref, buf, sem, *, rt, nt, ne):
    e = pl.program_id(0)
    t = pl.program_id(1)

    step = e * nt + t
    slot = step % 2

    def start_rows(first, s):
        def issue(r, carry):
            tok = idx_ref[first + r]
            pltpu.make_async_copy(h_hbm.at[pl.ds(tok, 1)], buf.at[s, pl.ds(r, 1)], sem.at[s]).start()
            return carry

        lax.fori_loop(0, rt, issue, 0, unroll=ISSUE_UNROLL)

    @pl.when(step == 0)
    def _():
        start_rows(0, 0)

    @pl.when(step + 1 < ne * nt)
    def _():
        start_rows((step + 1) * rt, 1 - slot)

    pltpu.make_async_copy(h_hbm.at[pl.ds(0, rt)], buf.at[slot], sem.at[slot]).wait()
    o_ref[0] = _norm2(buf[slot], g2_ref, sh_ref, sc_ref).astype(BF16)


def _gather_rows(x, idx, g2, shift, scale):
    T, D = x.shape
    E, cap = idx.shape
    rt = min(ROW_TILE, cap)
    nt = cap // rt
    vec = pl.BlockSpec((1, D), lambda e, t, idx: (0, 0))
    return pl.pallas_call(
        functools.partial(_gather_kernel, rt=rt, nt=nt, ne=E),
        grid_spec=pltpu.PrefetchScalarGridSpec(
            num_scalar_prefetch=1,
            grid=(E, nt),
            in_specs=[pl.BlockSpec(memory_space=pl.ANY), vec, vec, vec],
            out_specs=pl.BlockSpec((1, rt, D), lambda e, t, idx: (e, t, 0)),
            scratch_shapes=[pltpu.VMEM((2, rt, D), F32), pltpu.SemaphoreType.DMA((2,))]),
        out_shape=jax.ShapeDtypeStruct((E, cap, D), BF16),
        compiler_params=_params("arbitrary", "arbitrary"),
    )(idx.reshape(-1), x, g2, shift, scale)


def _combine_kernel(idx_ref, ye_ref, x_in, x_out, buf, gsem, ssem, *, rt, nt):
    del x_in
    e = pl.program_id(0)
    t = pl.program_id(1)
    slot = t % 2

    def start_rows(tile, s, gather):
        first = (e * nt + tile) * rt

        def body(r, carry):
            tok = idx_ref[first + r]
            hbm_row = x_out.at[pl.ds(tok, 1)]
            vmem_row = buf.at[s, pl.ds(r, 1)]
            if gather:
                pltpu.make_async_copy(hbm_row, vmem_row, gsem.at[s]).start()
            else:
                pltpu.make_async_copy(vmem_row, hbm_row, ssem.at[s]).start()
            return carry

        lax.fori_loop(0, rt, body, 0, unroll=ISSUE_UNROLL)

    def wait_rows(s, gather):
        if gather:
            pltpu.make_async_copy(x_out.at[pl.ds(0, rt)], buf.at[s], gsem.at[s]).wait()
        else:
            pltpu.make_async_copy(buf.at[s], x_out.at[pl.ds(0, rt)], ssem.at[s]).wait()

    @pl.when(t == 0)
    def _():
        start_rows(0, 0, True)

    wait_rows(slot, True)

    @pl.when(t + 1 < nt)
    def _():
        @pl.when(t >= 1)
        def _():
            wait_rows(1 - slot, False)

        start_rows(t + 1, 1 - slot, True)

    buf[slot] = buf[slot] + ye_ref[0]
    start_rows(t, slot, False)

    @pl.when(t == nt - 1)
    def _():
        wait_rows(slot, False)
        if nt > 1:
            wait_rows(1 - slot, False)


def _combine(x, ye, idx):
    E, cap, D = ye.shape
    rt = min(ROW_TILE, cap)
    nt = cap // rt
    return pl.pallas_call(
        functools.partial(_combine_kernel, rt=rt, nt=nt),
        grid_spec=pltpu.PrefetchScalarGridSpec(
            num_scalar_prefetch=1,
            grid=(E, nt),
            in_specs=[pl.BlockSpec((1, rt, D), lambda e, t, idx: (e, t, 0)),
                      pl.BlockSpec(memory_space=pl.ANY)],
            out_specs=pl.BlockSpec(memory_space=pl.ANY),
            scratch_shapes=[pltpu.VMEM((2, rt, D), F32), pltpu.SemaphoreType.DMA((2,)),
                            pltpu.SemaphoreType.DMA((2,))]),
        out_shape=jax.ShapeDtypeStruct(x.shape, F32),
        input_output_aliases={2: 0},
        compiler_params=_params("arbitrary", "arbitrary"),
    )(idx.reshape(-1), ye, x)


def _moe_kernel(*refs, nj, has_ctx):
    it = iter(refs)
    xs_ref, tw_ref, gate_ref = next(it), next(it), next(it)
    xc_ref, twc_ref, gatec_ref = (next(it), next(it), next(it)) if has_ctx else (None, None, None)
    wg_ref, wu_ref, wd_ref = next(it), next(it), next(it)
    ye_ref = next(it)
    yc_ref = next(it) if has_ctx else None
    hid_scr = next(it)
    hidc_scr = next(it) if has_ctx else None
    s = pl.program_id(1)
    tf = wg_ref.shape[3]

    def row_chunks(n_rows):
        step = min(MOE_ROW_CHUNK, n_rows)
        return [slice(m * step, (m + 1) * step) for m in range(n_rows // step)]

    def up(x_ref, h_scr):
        wg = wg_ref[0, 0].astype(BF16)
        wu = wu_ref[0, 0].astype(BF16)
        for rs in row_chunks(x_ref.shape[1]):
            x = x_ref[0, rs]
            a = _dot(x, wg)
            h_scr[s, rs] = ((a * jax.nn.sigmoid(a)) * _dot(x, wu)).astype(BF16)

    def down(h_scr, y_ref, w_ref, g_ref):
        wd = wd_ref[0, 0].astype(BF16)
        for rs in row_chunks(y_ref.shape[1]):
            acc = _dot(h_scr[0, rs], wd[0:tf])
            for j in range(1, nj):
                acc = acc + _dot(h_scr[j, rs], wd[j * tf:(j + 1) * tf])
            y_ref[0, rs] = acc * w_ref[0, rs] * g_ref[...]

    @pl.when(s < nj)
    def _():
        up(xs_ref, hid_scr)
        if has_ctx:
            up(xc_ref, hidc_scr)

    @pl.when(s >= nj)
    def _():
        down(hid_scr, ye_ref, tw_ref, gate_ref)
        if has_ctx:
            down(hidc_scr, yc_ref, twc_ref, gatec_ref)


def _moe(xs, tw, gate, layer, w_gate, w_up, w_down, ctx=None):
    E, cap, D = xs.shape
    FF = w_gate.shape[3]
    tf = tn = MOE_TILE
    nj, nn = FF // tf, D // tn
    has_ctx = ctx is not None
    jj = lambda s: jnp.minimum(s, nj - 1)
    nc = lambda s: jnp.maximum(s - nj, 0)
    gvec = pl.BlockSpec((1, tn), lambda e, s: (0, nc(s)))
    args = [xs, tw, gate]
    specs = [pl.BlockSpec((1, cap, D), lambda e, s: (e, 0, 0)),
             pl.BlockSpec((1, cap, 1), lambda e, s: (e, 0, 0)), gvec]
    outs = [pl.BlockSpec((1, cap, tn), lambda e, s: (e, 0, nc(s)))]
    shapes = [jax.ShapeDtypeStruct((E, cap, D), F32)]
    scratch = [pltpu.VMEM((nj, cap, tf), BF16)]
    if has_ctx:
        xc, twc, gatec = ctx
        cc = xc.shape[1]
        args += [xc, twc, gatec]
        specs += [pl.BlockSpec((1, cc, D), lambda e, s: (e, 0, 0)),
                  pl.BlockSpec((1, cc, 1), lambda e, s: (e, 0, 0)), gvec]
        outs.append(pl.BlockSpec((1, cc, tn), lambda e, s: (e, 0, nc(s))))
        shapes.append(jax.ShapeDtypeStruct((E, cc, D), F32))
        scratch.append(pltpu.VMEM((nj, cc, tf), BF16))
    args += [w_gate, w_up, w_down]
    specs += [pl.BlockSpec((1, 1, D, tf), lambda e, s: (layer, e, 0, jj(s))),
              pl.BlockSpec((1, 1, D, tf), lambda e, s: (layer, e, 0, jj(s))),
              pl.BlockSpec((1, 1, FF, tn), lambda e, s: (layer, e, 0, nc(s)))]
    res = pl.pallas_call(
        functools.partial(_moe_kernel, nj=nj, has_ctx=has_ctx),
        grid=(E, nj + nn),
        in_specs=specs,
        out_specs=outs,
        out_shape=shapes,
        scratch_shapes=scratch,
        compiler_params=_params("arbitrary", "arbitrary"),
    )(*args)
    return res if has_ctx else (res[0], None)


def _select_kernel(a_ref, idx_ref, *, cap):
    a = a_ref[0]
    R = a.shape[0]

    def count(mask):
        return jnp.sum(jnp.where(mask, 1.0, 0.0))

    def bisect(_, bounds):
        lo, hi = bounds
        mid = 0.5 * (lo + hi)
        enough = count(a >= mid) >= cap
        return jnp.where(enough, mid, lo), jnp.where(enough, hi, mid)

    lo, hi = lax.fori_loop(0, SELECT_BISECTIONS, bisect, (jnp.float32(0.0), jnp.float32(2.0)))
    above = a >= hi
    equal = jnp.logical_and(a >= lo, a < hi)
    need = cap - count(above)

    li = lax.broadcasted_iota(jnp.int32, (LANES, LANES), 0)
    lj = lax.broadcasted_iota(jnp.int32, (LANES, LANES), 1)
    upper = jnp.where(li <= lj, 1.0, 0.0).astype(BF16)
    ones = jnp.ones((LANES, LANES), BF16)
    ri = lax.broadcasted_iota(jnp.int32, (R, R), 0)
    rj = lax.broadcasted_iota(jnp.int32, (R, R), 1)
    earlier_rows = jnp.where(rj < ri, 1.0, 0.0).astype(BF16)

    def rank(mask):
        x = jnp.where(mask, 1.0, 0.0)
        xb = x.astype(BF16)
        return _dot(xb, upper) - x + _dot(earlier_rows, _dot(xb, ones).astype(BF16))

    take = jnp.logical_or(above, jnp.logical_and(equal, rank(equal) < need))

    x = jnp.where(take, 1.0, 0.0)
    xb = x.astype(BF16)
    counts = rank(take) + x
    row_tot = _dot_nt(jnp.ones((SUBLANES, LANES), BF16), xb)[0:1]
    ui = lax.broadcasted_iota(jnp.int32, (R, R), 0)
    uj = lax.broadcasted_iota(jnp.int32, (R, R), 1)
    row_incl = _dot(row_tot.astype(BF16), jnp.where(ui <= uj, 1.0, 0.0).astype(BF16))
    row_excl = row_incl - row_tot
    slot = lax.broadcasted_iota(jnp.int32, (cap, R), 0).astype(F32)
    holds = jnp.logical_and(row_excl <= slot, slot < row_incl)
    hb = jnp.where(holds, 1.0, 0.0).astype(BF16)
    hi = jnp.floor(counts * (1.0 / LANES))
    lo = counts - hi * LANES
    row_counts = _dot(hb, hi.astype(BF16)) * LANES + _dot(hb, lo.astype(BF16))
    slot_l = lax.broadcasted_iota(jnp.int32, (cap, LANES), 0).astype(F32)
    in_row = jnp.sum(jnp.where(row_counts <= slot_l, 1.0, 0.0), axis=1, keepdims=True)
    rows_before = jnp.sum(jnp.where(row_incl <= slot, 1.0, 0.0), axis=1, keepdims=True)
    idx_ref[0] = (rows_before * LANES + in_row).astype(jnp.int32)


def _select(aff, cap):
    T = aff.shape[0]
    R = pl.cdiv(pl.cdiv(T, LANES), LANES) * LANES
    a = jnp.full((N_EXPERTS, R * LANES), -1.0, F32).at[:, :T].set(aff[:, :N_EXPERTS].T)
    idx = pl.pallas_call(
        functools.partial(_select_kernel, cap=cap),
        grid=(N_EXPERTS,),
        in_specs=[pl.BlockSpec((1, R, LANES), lambda e: (e, 0, 0))],
        out_specs=pl.BlockSpec((1, cap, 1), lambda e: (e, 0, 0)),
        out_shape=jax.ShapeDtypeStruct((N_EXPERTS, cap, 1), jnp.int32),
        compiler_params=_params("arbitrary"),
    )(a.reshape(N_EXPERTS, R, LANES))
    return idx.reshape(N_EXPERTS, cap)


def _route(aff, x, g2, shift, scale):
    cap = CAPACITY * aff.shape[0] // N_EXPERTS
    top_idx = _select(aff, cap)
    top_w = jnp.take_along_axis(aff[:, :N_EXPERTS].T, top_idx, axis=1)
    return _gather_rows(x, top_idx, g2, shift, scale), top_w[..., None], top_idx


def _reorder_w_in(w):
    D = w.shape[0]
    sizes = (GLA_QK, GLA_QK, GLA_V, GLA_V, GLA_GATE_RANK, GLA_GATE_RANK, NA_W, NA_W, NA_W, LRU_WIDTH, LRU_WIDTH)
    parts, o = [], 0
    for s in sizes:
        parts.append(w[:, o:o + s])
        o += s
    gq, gk, gv, gr, lf, lb, nq, nk, nv, lx, ly = parts
    pad = jnp.zeros((D, LANES - 2 * GLA_GATE_RANK), w.dtype)
    main = jnp.concatenate([nq, nk, nv, gv, gr, lx, ly, gq, gk], axis=1).astype(BF16)
    return main, jnp.concatenate([lf, lb, pad], axis=1).astype(BF16)


def kernel(x, c, ctx, c_ctx, norm1_g, norm2_g, w_mod, b_mod, w_in, gla_w_gate, gla_b_gate, gla_norm_g, na_q_norm_g, na_k_norm_g, na_rpb, lru_conv_w, lru_conv_b, lru_w_a, lru_b_a, lru_w_i, lru_b_i, lru_lambda, w_out, w_router, w_exp_gate, w_exp_up, w_exp_down):
    B, T, D = x.shape
    Lc = ctx.shape[1]
    depth = w_in.shape[0]
    assert B == 1 and T % (NA_ROWS_PER_STEP * GRID_W) == 0 and T // GRID_W >= NA_UNION_ROWS
    xs = x[0]
    cs = ctx[0]
    mod = _modulation(jnp.stack([c[0], c_ctx]), w_mod, b_mod)
    cos, sin = _rope_tables(T)
    tb_l, tb_c = min(512, T), min(512, Lc)
    tri = {(tb, rev): _tri_matrix(tb, rev) for tb in {tb_l, tb_c} for rev in (False, True)}
    row = lambda v: v.reshape(1, -1)

    for l in range(depth):
        ctx_out = l < depth - 1
        mx = [row(m) for m in jnp.split(mod[l, 0], 6)]
        mc = [row(m) for m in jnp.split(mod[l, 1], 6)]
        w_main, w_lr = _reorder_w_in(w_in[l])
        g1 = row(norm1_g[l])
        zx, zx_lr = _inproj(xs, g1, mx[0], mx[1], w_main, w_lr)
        zc, zc_lr = _inproj(cs, g1, mc[0], mc[1], w_main, w_lr)

        gla_dir = []
        for d in range(2):
            wg = jnp.zeros((LANES, GLA_QK), F32).at[d * GLA_GATE_RANK:(d + 1) * GLA_GATE_RANK].set(gla_w_gate[l, d])
            gla_dir.append((wg, row(gla_b_gate[l, d])))
        ng = row(gla_norm_g[l])
        zero_state = jnp.zeros((GLA_HEADS, GLA_DV, GLA_DK), F32)
        oc_f, sc_f = _gla_pass(zc, zc_lr, None, None, *gla_dir[0], tri[(tb_c, False)], zero_state, reverse=False)
        gla_c, sc_b = _gla_pass(zc, zc_lr, None, None, *gla_dir[1], tri[(tb_c, True)], zero_state, reverse=True,
                                oprev=oc_f, norm_g=ng)
        ol_f, _ = _gla_pass(zx, zx_lr, cos, sin, *gla_dir[0], tri[(tb_l, False)], sc_f, reverse=False)
        gla_x, _ = _gla_pass(zx, zx_lr, cos, sin, *gla_dir[1], tri[(tb_l, True)], sc_b, reverse=True,
                             oprev=ol_f, norm_g=ng)

        qg, kg = row(na_q_norm_g[l]), row(na_k_norm_g[l])
        ql, kl, vl = _na_prep(zx, qg, kg)
        qc, kc, vc = _na_prep(zc, qg, kg)
        na_x = _na_latent(ql, kl, vl, kc, vc, _na_bias(na_rpb[l]))

        lru_dir = []
        for d in range(2):
            lru_dir.append(dict(conv_w=lru_conv_w[l], conv_b=row(lru_conv_b[l]),
                                w_a=lru_w_a[l, d].astype(BF16), b_a=row(lru_b_a[l, d]),
                                w_i=lru_w_i[l, d].astype(BF16), b_i=row(lru_b_i[l, d]),
                                lam=row(lru_lambda[l, d])))
        zero_h = jnp.zeros((1, LRU_WIDTH), F32)
        hc_f, fin_f = _lru_pass(zc, lru_dir[0], zero_h, reverse=False)
        lru_c, fin_b = _lru_pass(zc, lru_dir[1], zero_h, reverse=True, hprev=hc_f)
        hl_f, _ = _lru_pass(zx, lru_dir[0], fin_f[0:1], reverse=False)
        lru_x, _ = _lru_pass(zx, lru_dir[1], fin_b[0:1], reverse=True, hprev=hl_f)

        wo = w_out[l].astype(BF16)
        wr = jnp.zeros((D, LANES), BF16).at[:, :N_EXPERTS].set(w_router[l].astype(BF16))
        g2 = row(norm2_g[l])
        x1, affx = _outproj(gla_x, na_x, lru_x, xs, wo, mx[2], g2, mx[3], mx[4], wr)
        xg, twx, idxx = _route(affx, x1, g2, mx[3], mx[4])
        if ctx_out:
            na_c = _na_context(qc, kc, vc)
            c1, affc = _outproj(gla_c, na_c, lru_c, cs, wo, mc[2], g2, mc[3], mc[4], wr)
            cg, twc, idxc = _route(affc, c1, g2, mc[3], mc[4])
            ye, yc = _moe(xg, twx, mx[5], l, w_exp_gate, w_exp_up, w_exp_down, ctx=(cg, twc, mc[5]))
            cs = _combine(c1, yc, idxc)
        else:
            ye, _ = _moe(xg, twx, mx[5], l, w_exp_gate, w_exp_up, w_exp_down)
        xs = _combine(x1, ye, idxx)
    return xs[None]
```

```python
import functools

import numpy as np
import jax
import jax.numpy as jnp
from jax import lax
from jax.experimental import pallas as pl
from jax.experimental.pallas import tpu as pltpu

F32 = jnp.float32
BF16 = jnp.bfloat16

GRID_W = 64
EPS = 1e-6
GLA_HEADS = 4
GLA_DK = 64
GLA_DV = 128
GLA_GATE_RANK = 16
GLA_TAU = 16.0
GLA_CHUNK = 64
ROPE_BASE = 10000.0
NA_HEADS = 8
NA_HD = 128
NA_KH = 8
NA_KW = 16
LRU_WIDTH = 512
LRU_BLOCKS = 4
LRU_CONV = 4
LRU_C = 8.0
N_EXPERTS = 16
CAPACITY = 2

GLA_QK = GLA_HEADS * GLA_DK
GLA_V = GLA_HEADS * GLA_DV
NA_W = NA_HEADS * NA_HD

LANES = 128
SUBLANES = 8
VMEM_LIMIT = 56 * 1024 * 1024
NEG = -1e30
ROW_TILE = 512
ISSUE_UNROLL = 8
SELECT_QUARTERINGS = 24
MOE_TILE = 256
MOE_ROW_CHUNK = 512
LRU_HALO = 16
NA_ROWS_PER_STEP = 8
NA_UNION_ROWS = 16
NA_Q_CHUNK = 256
NA_PAIR_OFFSET = NA_UNION_ROWS - NA_KH
NA_PAIR_ENTRIES = NA_UNION_ROWS + 2 * NA_KH - 2

Z_NAQ, Z_NAK, Z_NAV = 0, NA_W, 2 * NA_W
Z_GV = 3 * NA_W
Z_GR = Z_GV + GLA_V
Z_LX = Z_GR + GLA_V
Z_LY = Z_LX + LRU_WIDTH
Z_GQ = Z_LY + LRU_WIDTH
Z_GK = Z_GQ + GLA_QK
Z_COLS = Z_GK + GLA_QK
Z_CHUNK = 512


def _params(*sem):
    return pltpu.CompilerParams(dimension_semantics=sem, vmem_limit_bytes=VMEM_LIMIT)


def _dot(a, b):
    return jnp.dot(a, b, preferred_element_type=F32)


def _dot_nt(a, b):
    return lax.dot_general(a, b, (((1,), (1,)), ((), ())), preferred_element_type=F32)


def _dot_tn(a, b):
    return lax.dot_general(a, b, (((0,), (0,)), ((), ())), preferred_element_type=F32)


def _split2(a):
    hi = a.astype(BF16)
    lo = (a - hi.astype(F32)).astype(BF16)
    return hi, lo


def _dot_hi(a, b):
    ah, al = _split2(a)
    bh, bl = _split2(b)
    return _dot(ah, bh) + _dot(al, bh) + _dot(ah, bl)


def _rms(x):
    return x * lax.rsqrt(jnp.mean(x * x, axis=-1, keepdims=True) + EPS)


def _norm2(x, g_ref, sh_ref, sc_ref):
    return (_rms(x) * g_ref[...]) * (1.0 + sc_ref[...]) + sh_ref[...]


def _log_sigmoid(z):
    return jnp.minimum(z, 0.0) - jnp.log1p(jnp.exp(-jnp.abs(z)))


def _mod_kernel(cb_ref, w_ref, b_ref, o_ref):
    tn = w_ref.shape[2]
    s = [cb_ref[v] * jax.nn.sigmoid(cb_ref[v]) for v in range(2)]
    for j in range(tn // LANES):
        cs = slice(j * LANES, (j + 1) * LANES)
        wj = w_ref[0, :, cs]
        for v in range(2):
            o_ref[0, v:v + 1, cs] = jnp.sum(wj * s[v], axis=0, keepdims=True) + b_ref[0, :, cs]


def _modulation(c2, w_mod, b_mod):
    L, D, N6 = w_mod.shape
    tn = 1024
    cb = jnp.broadcast_to(c2[:, :, None], (2, D, LANES))
    return pl.pallas_call(
        _mod_kernel,
        grid=(L, N6 // tn),
        in_specs=[pl.BlockSpec((2, D, LANES), lambda l, j: (0, 0, 0)),
                  pl.BlockSpec((1, D, tn), lambda l, j: (l, 0, j)),
                  pl.BlockSpec((1, 1, tn), lambda l, j: (l, 0, j))],
        out_specs=pl.BlockSpec((1, 2, tn), lambda l, j: (l, 0, j)),
        out_shape=jax.ShapeDtypeStruct((L, 2, N6), F32),
        compiler_params=_params("arbitrary", "arbitrary"),
    )(cb, w_mod, b_mod.reshape(L, 1, N6))


def _inproj_kernel(x_ref, g_ref, sh_ref, sc_ref, w_ref, wlr_ref, o_ref, olr_ref):
    h = _norm2(x_ref[...], g_ref, sh_ref, sc_ref).astype(BF16)
    for n in range(w_ref.shape[1] // Z_CHUNK):
        cs = slice(n * Z_CHUNK, (n + 1) * Z_CHUNK)
        o_ref[:, cs] = _dot(h, w_ref[:, cs]).astype(BF16)
    olr_ref[...] = _dot(h, wlr_ref[...])


def _inproj(x, gain, shift, scale, w, wlr):
    T, D = x.shape
    NZ = w.shape[1]
    tm = min(512, T)
    vec = pl.BlockSpec((1, D), lambda i: (0, 0))
    resident = lambda a: pl.BlockSpec(a.shape, lambda i: (0, 0), pipeline_mode=pl.Buffered(1))
    return pl.pallas_call(
        _inproj_kernel,
        grid=(T // tm,),
        in_specs=[pl.BlockSpec((tm, D), lambda i: (i, 0)), vec, vec, vec, resident(w), resident(wlr)],
        out_specs=[pl.BlockSpec((tm, NZ), lambda i: (i, 0)), pl.BlockSpec((tm, LANES), lambda i: (i, 0))],
        out_shape=[jax.ShapeDtypeStruct((T, NZ), BF16), jax.ShapeDtypeStruct((T, LANES), F32)],
        compiler_params=_params("arbitrary"),
    )(x, gain, shift, scale, w, wlr)


def _gla_kernel(*refs, reverse, rope, final, tb):
    it = iter(refs)
    q_ref, k_ref, v_ref, lr_ref = next(it), next(it), next(it), next(it)
    cos_ref, sin_ref = (next(it), next(it)) if rope else (None, None)
    wg_ref, bg_ref, tri_ref, s0_ref = next(it), next(it), next(it), next(it)
    oprev_ref, r_ref, ng_ref = (next(it), next(it), next(it)) if final else (None, None, None)
    o_ref, sfin_ref, st_scr = next(it), next(it), next(it)

    @pl.when(pl.program_id(0) == 0)
    def _():
        st_scr[...] = s0_ref[...]

    q = q_ref[...].astype(F32)
    k = k_ref[...].astype(F32)
    if rope:
        lane = lax.broadcasted_iota(jnp.int32, q.shape, 1)
        first = (lane % GLA_DK) < (GLA_DK // 2)
        cos, sin = cos_ref[...], sin_ref[...]

        def rot(t):
            return jnp.where(first, pltpu.roll(t, GLA_QK - GLA_DK // 2, 1), pltpu.roll(t, GLA_DK // 2, 1))

        q = q * cos + rot(q) * sin
        k = k * cos + rot(k) * sin
    q = q * (GLA_DK ** -0.5)

    g = _log_sigmoid(_dot_hi(lr_ref[...], wg_ref[...]) + bg_ref[...]) * (1.0 / GLA_TAU)
    g_hi = g.astype(BF16)
    g_r1 = g - g_hi.astype(F32)
    g_mid = g_r1.astype(BF16)
    g_lo = (g_r1 - g_mid.astype(F32)).astype(BF16)
    tri = tri_ref[...]
    gc = _dot(tri, g_hi) + _dot(tri, g_mid) + _dot(tri, g_lo)

    C = GLA_CHUNK
    ri = lax.broadcasted_iota(jnp.int32, (C, C), 0)
    ci = lax.broadcasted_iota(jnp.int32, (C, C), 1)
    causal = (ci >= ri) if reverse else (ci <= ri)
    nchunk = tb // C
    for c in (range(nchunk - 1, -1, -1) if reverse else range(nchunk)):
        sl = slice(c * C, (c + 1) * C)
        gcc = gc[sl]
        gt = gcc[0:1] if reverse else gcc[C - 1:C]
        qe = q[sl] * jnp.exp(gcc)
        ke = k[sl] * jnp.exp(-gcc)
        kd = k[sl] * jnp.exp(gt - gcc)
        dec = jnp.exp(gt)
        for h in range(GLA_HEADS):
            hs = slice(h * GLA_DK, (h + 1) * GLA_DK)
            vs = slice(h * GLA_DV, (h + 1) * GLA_DV)
            qh = qe[:, hs].astype(BF16)
            vh = v_ref[sl, vs]
            att = jnp.where(causal, _dot_nt(qh, ke[:, hs].astype(BF16)), 0.0)
            st = st_scr[h]
            o = _dot(att.astype(BF16), vh) + _dot_nt(qh, st.astype(BF16))
            st_scr[h] = st * dec[:, hs] + _dot_tn(vh, kd[:, hs].astype(BF16))
            if final:
                y = _rms(oprev_ref[sl, vs] + o) * ng_ref[...]
                rg = r_ref[sl, vs].astype(F32)
                o_ref[sl, vs] = (y * (rg * jax.nn.sigmoid(rg))).astype(o_ref.dtype)
            else:
                o_ref[sl, vs] = o
    sfin_ref[...] = st_scr[...]


def _gla_pass(z, zlr, cos, sin, wg, bg, tri, s0, *, reverse, oprev=None, norm_g=None):
    T = z.shape[0]
    tb = tri.shape[0]
    nb = T // tb
    rope = cos is not None
    final = oprev is not None
    blk = (lambda i: nb - 1 - i) if reverse else (lambda i: i)

    def col(width, start):
        return pl.BlockSpec((tb, width), lambda i: (blk(i), start // width))

    const2 = lambda shape: pl.BlockSpec(shape, lambda i: (0, 0))
    state = pl.BlockSpec((GLA_HEADS, GLA_DV, GLA_DK), lambda i: (0, 0, 0))
    args = [z, z, z, zlr]
    specs = [col(GLA_QK, Z_GQ), col(GLA_QK, Z_GK), col(GLA_V, Z_GV), col(LANES, 0)]
    if rope:
        args += [cos, sin]
        specs += [col(GLA_QK, 0), col(GLA_QK, 0)]
    args += [wg, bg, tri, s0]
    specs += [const2((LANES, GLA_QK)), const2((1, GLA_QK)), const2((tb, tb)), state]
    if final:
        args += [oprev, z, norm_g]
        specs += [col(GLA_V, 0), col(GLA_V, Z_GR), const2((1, GLA_DV))]
    return pl.pallas_call(
        functools.partial(_gla_kernel, reverse=reverse, rope=rope, final=final, tb=tb),
        grid=(nb,),
        in_specs=specs,
        out_specs=[col(GLA_V, 0), state],
        out_shape=[jax.ShapeDtypeStruct((T, GLA_V), BF16 if final else F32),
                   jax.ShapeDtypeStruct((GLA_HEADS, GLA_DV, GLA_DK), F32)],
        scratch_shapes=[pltpu.VMEM((GLA_HEADS, GLA_DV, GLA_DK), F32)],
        compiler_params=_params("arbitrary"),
    )(*args)


def _tri_matrix(tb, reverse):
    i = np.arange(tb)
    same = (i[:, None] // GLA_CHUNK) == (i[None, :] // GLA_CHUNK)
    order = (i[None, :] >= i[:, None]) if reverse else (i[None, :] <= i[:, None])
    return jnp.asarray(same & order, BF16)


def _rope_tables(T):
    pos = jnp.arange(T)
    row = (pos // GRID_W).astype(F32)
    col = (pos % GRID_W).astype(F32)
    nf = GLA_DK // 4
    inv = ROPE_BASE ** (-jnp.arange(nf, dtype=F32) / nf)
    ang = jnp.concatenate([row[:, None] * inv, col[:, None] * inv], axis=-1)
    cos, sin = jnp.cos(ang), jnp.sin(ang)
    cos_full = jnp.tile(jnp.concatenate([cos, cos], axis=-1), (1, GLA_HEADS))
    sin_full = jnp.tile(jnp.concatenate([-sin, sin], axis=-1), (1, GLA_HEADS))
    return cos_full, sin_full


def _naprep_kernel(q_ref, k_ref, v_ref, qg_ref, kg_ref, qo_ref, ko_ref, vo_ref):
    for h in range(NA_HEADS):
        hs = slice(h * NA_HD, (h + 1) * NA_HD)
        qo_ref[h] = (_rms(q_ref[:, hs].astype(F32)) * qg_ref[...] * (NA_HD ** -0.5)).astype(BF16)
        ko_ref[h] = (_rms(k_ref[:, hs].astype(F32)) * kg_ref[...]).astype(BF16)
        vo_ref[h] = v_ref[:, hs]


def _na_prep(z, qg, kg):
    T = z.shape[0]
    tm = min(512, T)
    col = lambda c: pl.BlockSpec((tm, NA_W), lambda i: (i, c))
    vec = pl.BlockSpec((1, NA_HD), lambda i: (0, 0))
    out = pl.BlockSpec((NA_HEADS, tm, NA_HD), lambda i: (0, i, 0))
    shp = jax.ShapeDtypeStruct((NA_HEADS, T, NA_HD), BF16)
    return pl.pallas_call(
        _naprep_kernel,
        grid=(T // tm,),
        in_specs=[col(Z_NAQ // NA_W), col(Z_NAK // NA_W), col(Z_NAV // NA_W), vec, vec],
        out_specs=[out, out, out],
        out_shape=[shp, shp, shp],
        compiler_params=_params("arbitrary"),
    )(z, z, z, qg, kg)


def _na_kernel(q_ref, k_ref, v_ref, kc_ref, vc_ref, bias_ref, o_ref, *, n_rows):
    i = pl.program_id(1)
    W = GRID_W
    R = NA_ROWS_PER_STEP
    first_row = jnp.clip(i * R - NA_KH // 2, 0, n_rows - NA_UNION_ROWS)
    start = pl.multiple_of(first_row * W, W)
    kw = k_ref[0, pl.ds(start, NA_UNION_ROWS * W), :]
    vw = v_ref[0, pl.ds(start, NA_UNION_ROWS * W), :]
    kc = kc_ref[0]
    vc = vc_ref[0]
    upper = lax.broadcasted_iota(jnp.int32, (1, 2 * W), 1) // W

    def bias_rows(rr):
        r = i * R + rr
        lo = jnp.clip(r - NA_KH // 2, 0, n_rows - NA_KH) - r + (NA_KH - 1)
        blocks = []
        for m in range(NA_UNION_ROWS // 2):
            a_even = first_row + 2 * m - r + (NA_KH - 1)
            a = a_even + upper
            ok = jnp.logical_and(a >= lo, a < lo + NA_KH)
            blocks.append(jnp.where(ok, bias_ref[0, a_even + NA_PAIR_OFFSET], NEG))
        return jnp.concatenate(blocks, axis=1)

    nq = q_ref.shape[1]
    step = min(NA_Q_CHUNK, nq)
    for c in range(nq // step):
        qs = slice(c * step, (c + 1) * step)
        q = q_ref[0, qs, :]
        bias = jnp.concatenate([bias_rows(rr) for rr in range(c * step // W, (c + 1) * step // W)], axis=0)
        s_loc = _dot_nt(q, kw) + bias
        s_ctx = _dot_nt(q, kc)
        m = jnp.maximum(jnp.max(s_loc, axis=-1, keepdims=True), jnp.max(s_ctx, axis=-1, keepdims=True))
        p_loc = jnp.exp(s_loc - m)
        p_ctx = jnp.exp(s_ctx - m)
        den = jnp.sum(p_loc, axis=-1, keepdims=True) + jnp.sum(p_ctx, axis=-1, keepdims=True)
        o = _dot(p_loc.astype(BF16), vw) + _dot(p_ctx.astype(BF16), vc)
        o_ref[qs, :] = (o / den).astype(BF16)


def _na_latent(q, k, v, kc, vc, bias):
    H, T, hd = q.shape
    Lc = kc.shape[1]
    n_rows = T // GRID_W
    R = NA_ROWS_PER_STEP
    nblk = n_rows // R
    whole = lambda n: pl.BlockSpec((1, n, hd), lambda h, i: (h, 0, 0))
    return pl.pallas_call(
        functools.partial(_na_kernel, n_rows=n_rows),
        grid=(H, nblk),
        in_specs=[pl.BlockSpec((1, R * GRID_W, hd), lambda h, i: (h, i, 0)),
                  whole(T), whole(T), whole(Lc), whole(Lc),
                  pl.BlockSpec((1,) + bias.shape[1:], lambda h, i: (h, 0, 0, 0))],
        out_specs=pl.BlockSpec((R * GRID_W, hd), lambda h, i: (i, h)),
        out_shape=jax.ShapeDtypeStruct((T, H * hd), BF16),
        compiler_params=_params("arbitrary", "arbitrary"),
    )(q, k, v, kc, vc, bias)


def _ctxattn_kernel(q_ref, k_ref, v_ref, o_ref):
    s = _dot_nt(q_ref[0], k_ref[0])
    p = jnp.exp(s - jnp.max(s, axis=-1, keepdims=True))
    o = _dot(p.astype(BF16), v_ref[0]) / jnp.sum(p, axis=-1, keepdims=True)
    o_ref[...] = o.astype(BF16)


def _na_context(q, k, v):
    H, Lc, hd = q.shape
    spec = pl.BlockSpec((1, Lc, hd), lambda h: (h, 0, 0))
    return pl.pallas_call(
        _ctxattn_kernel,
        grid=(H,),
        in_specs=[spec, spec, spec],
        out_specs=pl.BlockSpec((Lc, hd), lambda h: (0, h)),
        out_shape=jax.ShapeDtypeStruct((Lc, H * hd), BF16),
        compiler_params=_params("arbitrary"),
    )(q, k, v)


def _na_bias(rpb):
    W = GRID_W
    cols = np.arange(W)
    col_start = np.clip(cols - NA_KW // 2, 0, W - NA_KW)
    col_ok = (cols[None, :] >= col_start[:, None]) & (cols[None, :] < col_start[:, None] + NA_KW)
    dc = np.clip(cols[None, :] - cols[:, None] + (NA_KW - 1), 0, 2 * NA_KW - 2)
    tab = jnp.where(col_ok[None, None], rpb[:, :, dc], NEG).astype(F32)
    n_tab = 2 * NA_KH - 1
    pad_lo = jnp.full((rpb.shape[0], NA_PAIR_OFFSET, W, W), NEG, F32)
    pad_hi = jnp.full((rpb.shape[0], NA_PAIR_ENTRIES + 1 - NA_PAIR_OFFSET - n_tab, W, W), NEG, F32)
    ext = jnp.concatenate([pad_lo, tab, pad_hi], axis=1)
    return jnp.concatenate([ext[:, :-1], ext[:, 1:]], axis=-1)


def _lru_kernel(*refs, reverse, final, tb, nblk):
    it = iter(refs)
    x_ref, xp_ref, xn_ref = next(it), next(it), next(it)
    cw_ref, cb_ref, wa_ref, ba_ref, wi_ref, bi_ref, lam_ref, h0_ref = (next(it) for _ in range(8))
    hprev_ref, y_ref = (next(it), next(it)) if final else (None, None)
    o_ref, hfin_ref = next(it), next(it)
    xx_scr, a_scr, b_scr, carry_scr = next(it), next(it), next(it), next(it)

    i = pl.program_id(0)
    blk = (nblk - 1 - i) if reverse else i
    S = SUBLANES
    Wd = LRU_WIDTH

    @pl.when(i == 0)
    def _():
        carry_scr[...] = jnp.broadcast_to(h0_ref[...], (S, Wd))

    H = LRU_HALO
    xx_scr[0:H] = jnp.where(blk > 0, xp_ref[...].astype(F32), 0.0)
    xx_scr[H:H + tb] = x_ref[...].astype(F32)
    xx_scr[H + tb:2 * H + tb] = jnp.where(blk < nblk - 1, xn_ref[...].astype(F32), 0.0)
    xc = cb_ref[...]
    for j in range(LRU_CONV):
        xc = xc + cw_ref[j:j + 1] * xx_scr[H - 1 + j:H - 1 + j + tb]

    bw = Wd // LRU_BLOCKS
    ra, ri = [], []
    for n in range(LRU_BLOCKS):
        xb = xc[:, n * bw:(n + 1) * bw].astype(BF16)
        ra.append(_dot(xb, wa_ref[n]))
        ri.append(_dot(xb, wi_ref[n]))
    rg = jax.nn.sigmoid(jnp.concatenate(ra, axis=1) + ba_ref[...])
    ig = jax.nn.sigmoid(jnp.concatenate(ri, axis=1) + bi_ref[...])
    log_a = LRU_C * rg * _log_sigmoid(lam_ref[...])
    a = jnp.exp(log_a)
    a_scr[...] = a
    b_scr[...] = jnp.sqrt(-jnp.tanh(log_a) * (a * a + 1.0)) * (ig * xc)

    row = lax.broadcasted_iota(jnp.int32, (S, Wd), 0)
    ng = tb // S

    def body(gi, carry):
        g = (ng - 1 - gi) if reverse else gi
        off = pl.multiple_of(g * S, S)
        a = a_scr[pl.ds(off, S), :]
        b = b_scr[pl.ds(off, S), :]
        for s in (1, 2, 4):
            sh = (S - s) if reverse else s
            keep = (row < S - s) if reverse else (row >= s)
            b = jnp.where(keep, a * pltpu.roll(b, sh, 0) + b, b)
            a = jnp.where(keep, a * pltpu.roll(a, sh, 0), a)
        h = b + a * carry
        b_scr[pl.ds(off, S), :] = h
        return jnp.broadcast_to(h[0:1] if reverse else h[S - 1:S], (S, Wd))

    carry = lax.fori_loop(0, ng, body, carry_scr[...])
    carry_scr[...] = carry
    hfin_ref[...] = carry
    if final:
        o_ref[...] = ((hprev_ref[...] + b_scr[...]) * jax.nn.gelu(y_ref[...].astype(F32))).astype(o_ref.dtype)
    else:
        o_ref[...] = b_scr[...]


def _lru_pass(z, p, h0, *, reverse, hprev=None):
    T = z.shape[0]
    tb = min(512, T)
    nb = T // tb
    S = SUBLANES
    Wd = LRU_WIDTH
    final = hprev is not None
    blk = (lambda i: nb - 1 - i) if reverse else (lambda i: i)
    H = LRU_HALO
    gpb = tb // H
    xcol = Z_LX // Wd
    row = pl.BlockSpec((1, Wd), lambda i: (0, 0))
    wspec = pl.BlockSpec((LRU_BLOCKS, Wd // LRU_BLOCKS, Wd // LRU_BLOCKS), lambda i: (0, 0, 0))
    args = [z, z, z, p["conv_w"], p["conv_b"], p["w_a"], p["b_a"], p["w_i"], p["b_i"], p["lam"], h0]
    specs = [pl.BlockSpec((tb, Wd), lambda i: (blk(i), xcol)),
             pl.BlockSpec((H, Wd), lambda i: (jnp.maximum(blk(i) * gpb - 1, 0), xcol)),
             pl.BlockSpec((H, Wd), lambda i: (jnp.minimum((blk(i) + 1) * gpb, T // H - 1), xcol)),
             pl.BlockSpec((LRU_CONV, Wd), lambda i: (0, 0)), row, wspec, row, wspec, row, row, row]
    if final:
        args += [hprev, z]
        specs += [pl.BlockSpec((tb, Wd), lambda i: (blk(i), 0)),
                  pl.BlockSpec((tb, Wd), lambda i: (blk(i), Z_LY // Wd))]
    return pl.pallas_call(
        functools.partial(_lru_kernel, reverse=reverse, final=final, tb=tb, nblk=nb),
        grid=(nb,),
        in_specs=specs,
        out_specs=[pl.BlockSpec((tb, Wd), lambda i: (blk(i), 0)), pl.BlockSpec((S, Wd), lambda i: (0, 0))],
        out_shape=[jax.ShapeDtypeStruct((T, Wd), BF16 if final else F32),
                   jax.ShapeDtypeStruct((S, Wd), F32)],
        scratch_shapes=[pltpu.VMEM((tb + 2 * H, Wd), F32), pltpu.VMEM((tb, Wd), F32),
                        pltpu.VMEM((tb, Wd), F32), pltpu.VMEM((S, Wd), F32)],
        compiler_params=_params("arbitrary"),
    )(*args)


def _outproj_kernel(gla_ref, na_ref, lru_ref, x_ref, wo_ref, gate_ref, g2_ref, sh_ref, sc_ref,
                    wr_ref, xo_ref, aff_ref, mix_scr):
    mix_scr[:, 0:GLA_V] = gla_ref[...]
    mix_scr[:, GLA_V:GLA_V + NA_W] = na_ref[...]
    mix_scr[:, GLA_V + NA_W:] = lru_ref[...]
    xn = x_ref[...] + gate_ref[...] * _dot(mix_scr[...], wo_ref[...])
    xo_ref[...] = xn
    h = _norm2(xn, g2_ref, sh_ref, sc_ref)
    logits = _dot(h.astype(BF16), wr_ref[...])
    lane = lax.broadcasted_iota(jnp.int32, logits.shape, 1)
    logits = jnp.where(lane < N_EXPERTS, logits, NEG)
    e = jnp.exp(logits - jnp.max(logits, axis=-1, keepdims=True))
    aff_ref[...] = e / jnp.sum(e, axis=-1, keepdims=True)


def _outproj(gla, na, lru, x, wo, gate, g2, shift, scale, wr):
    T, D = x.shape
    tm = min(512, T)
    rows = lambda w: pl.BlockSpec((tm, w), lambda i: (i, 0))
    full = lambda a: pl.BlockSpec(a.shape, lambda i: (0, 0), pipeline_mode=pl.Buffered(1))
    vec = pl.BlockSpec((1, D), lambda i: (0, 0))
    return pl.pallas_call(
        _outproj_kernel,
        grid=(T // tm,),
        in_specs=[rows(GLA_V), rows(NA_W), rows(LRU_WIDTH), rows(D), full(wo), vec, vec, vec, vec, full(wr)],
        out_specs=[rows(D), rows(LANES)],
        out_shape=[jax.ShapeDtypeStruct((T, D), F32), jax.ShapeDtypeStruct((T, LANES), F32)],
        scratch_shapes=[pltpu.VMEM((tm, wo.shape[0]), BF16)],
        compiler_params=_params("arbitrary"),
    )(gla, na, lru, x, wo, gate, g2, shift, scale, wr)


def _gather_kernel(idx_ref, h_hbm, g2_ref, sh_ref, sc_ref, o_ref, buf, sem, *, rt, nt, ne):
    e = pl.program_id(0)
    t = pl.program_id(1)

    step = e * nt + t
    slot = step % 2

    def start_rows(first, s):
        def issue(r, carry):
            tok = idx_ref[first + r]
            pltpu.make_async_copy(h_hbm.at[pl.ds(tok, 1)], buf.at[s, pl.ds(r, 1)], sem.at[s]).start()
            return carry

        lax.fori_loop(0, rt, issue, 0, unroll=ISSUE_UNROLL)

    @pl.when(step == 0)
    def _():
        start_rows(0, 0)

    @pl.when(step + 1 < ne * nt)
    def _():
        start_rows((step + 1) * rt, 1 - slot)

    pltpu.make_async_copy(h_hbm.at[pl.ds(0, rt)], buf.at[slot], sem.at[slot]).wait()
    o_ref[0] = _norm2(buf[slot], g2_ref, sh_ref, sc_ref).astype(BF16)


def _gather_rows(x, idx, g2, shift, scale):
    T, D = x.shape
    E, cap = idx.shape
    rt = min(ROW_TILE, cap)
    nt = cap // rt
    vec = pl.BlockSpec((1, D), lambda e, t, idx: (0, 0))
    return pl.pallas_call(
        functools.partial(_gather_kernel, rt=rt, nt=nt, ne=E),
        grid_spec=pltpu.PrefetchScalarGridSpec(
            num_scalar_prefetch=1,
            grid=(E, nt),
            in_specs=[pl.BlockSpec(memory_space=pl.ANY), vec, vec, vec],
            out_specs=pl.BlockSpec((1, rt, D), lambda e, t, idx: (e, t, 0)),
            scratch_shapes=[pltpu.VMEM((2, rt, D), F32), pltpu.SemaphoreType.DMA((2,))]),
        out_shape=jax.ShapeDtypeStruct((E, cap, D), BF16),
        compiler_params=_params("arbitrary", "arbitrary"),
    )(idx.reshape(-1), x, g2, shift, scale)


def _combine_kernel(idx_ref, ye_ref, x_in, x_out, buf, gsem, ssem, *, rt, nt):
    del x_in
    e = pl.program_id(0)
    t = pl.program_id(1)
    slot = t % 2

    def start_rows(tile, s, gather):
        first = (e * nt + tile) * rt

        def body(r, carry):
            tok = idx_ref[first + r]
            hbm_row = x_out.at[pl.ds(tok, 1)]
            vmem_row = buf.at[s, pl.ds(r, 1)]
            if gather:
                pltpu.make_async_copy(hbm_row, vmem_row, gsem.at[s]).start()
            else:
                pltpu.make_async_copy(vmem_row, hbm_row, ssem.at[s]).start()
            return carry

        lax.fori_loop(0, rt, body, 0, unroll=ISSUE_UNROLL)

    def wait_rows(s, gather):
        if gather:
            pltpu.make_async_copy(x_out.at[pl.ds(0, rt)], buf.at[s], gsem.at[s]).wait()
        else:
            pltpu.make_async_copy(buf.at[s], x_out.at[pl.ds(0, rt)], ssem.at[s]).wait()

    @pl.when(t == 0)
    def _():
        start_rows(0, 0, True)

    wait_rows(slot, True)

    @pl.when(t + 1 < nt)
    def _():
        @pl.when(t >= 1)
        def _():
            wait_rows(1 - slot, False)

        start_rows(t + 1, 1 - slot, True)

    buf[slot] = buf[slot] + ye_ref[0]
    start_rows(t, slot, False)

    @pl.when(t == nt - 1)
    def _():
        wait_rows(slot, False)
        if nt > 1:
            wait_rows(1 - slot, False)


def _combine(x, ye, idx):
    E, cap, D = ye.shape
    rt = min(ROW_TILE, cap)
    nt = cap // rt
    return pl.pallas_call(
        functools.partial(_combine_kernel, rt=rt, nt=nt),
        grid_spec=pltpu.PrefetchScalarGridSpec(
            num_scalar_prefetch=1,
            grid=(E, nt),
            in_specs=[pl.BlockSpec((1, rt, D), lambda e, t, idx: (e, t, 0)),
                      pl.BlockSpec(memory_space=pl.ANY)],
            out_specs=pl.BlockSpec(memory_space=pl.ANY),
            scratch_shapes=[pltpu.VMEM((2, rt, D), F32), pltpu.SemaphoreType.DMA((2,)),
                            pltpu.SemaphoreType.DMA((2,))]),
        out_shape=jax.ShapeDtypeStruct(x.shape, F32),
        input_output_aliases={2: 0},
        compiler_params=_params("arbitrary", "arbitrary"),
    )(idx.reshape(-1), ye, x)


def _moe_kernel(*refs, nj, has_ctx):
    it = iter(refs)
    xs_ref, tw_ref, gate_ref = next(it), next(it), next(it)
    xc_ref, twc_ref, gatec_ref = (next(it), next(it), next(it)) if has_ctx else (None, None, None)
    wg_ref, wu_ref, wd_ref = next(it), next(it), next(it)
    ye_ref = next(it)
    yc_ref = next(it) if has_ctx else None
    hid_scr = next(it)
    hidc_scr = next(it) if has_ctx else None
    s = pl.program_id(1)
    tf = wg_ref.shape[3]

    def row_chunks(n_rows):
        step = min(MOE_ROW_CHUNK, n_rows)
        return [slice(m * step, (m + 1) * step) for m in range(n_rows // step)]

    def up(x_ref, h_scr):
        wg = wg_ref[0, 0].astype(BF16)
        wu = wu_ref[0, 0].astype(BF16)
        for rs in row_chunks(x_ref.shape[1]):
            x = x_ref[0, rs]
            a = _dot(x, wg)
            h_scr[s, rs] = ((a * jax.nn.sigmoid(a)) * _dot(x, wu)).astype(BF16)

    def down(h_scr, y_ref, w_ref, g_ref):
        wd = wd_ref[0, 0].astype(BF16)
        for rs in row_chunks(y_ref.shape[1]):
            acc = _dot(h_scr[0, rs], wd[0:tf])
            for j in range(1, nj):
                acc = acc + _dot(h_scr[j, rs], wd[j * tf:(j + 1) * tf])
            y_ref[0, rs] = acc * w_ref[0, rs] * g_ref[...]

    @pl.when(s < nj)
    def _():
        up(xs_ref, hid_scr)
        if has_ctx:
            up(xc_ref, hidc_scr)

    @pl.when(s >= nj)
    def _():
        down(hid_scr, ye_ref, tw_ref, gate_ref)
        if has_ctx:
            down(hidc_scr, yc_ref, twc_ref, gatec_ref)


def _moe(xs, tw, gate, layer, w_gate, w_up, w_down, ctx=None):
    E, cap, D = xs.shape
    FF = w_gate.shape[3]
    tf = tn = MOE_TILE
    nj, nn = FF // tf, D // tn
    has_ctx = ctx is not None
    jj = lambda s: jnp.minimum(s, nj - 1)
    nc = lambda s: jnp.maximum(s - nj, 0)
    gvec = pl.BlockSpec((1, tn), lambda e, s: (0, nc(s)))
    args = [xs, tw, gate]
    specs = [pl.BlockSpec((1, cap, D), lambda e, s: (e, 0, 0)),
             pl.BlockSpec((1, cap, 1), lambda e, s: (e, 0, 0)), gvec]
    outs = [pl.BlockSpec((1, cap, tn), lambda e, s: (e, 0, nc(s)))]
    shapes = [jax.ShapeDtypeStruct((E, cap, D), F32)]
    scratch = [pltpu.VMEM((nj, cap, tf), BF16)]
    if has_ctx:
        xc, twc, gatec = ctx
        cc = xc.shape[1]
        args += [xc, twc, gatec]
        specs += [pl.BlockSpec((1, cc, D), lambda e, s: (e, 0, 0)),
                  pl.BlockSpec((1, cc, 1), lambda e, s: (e, 0, 0)), gvec]
        outs.append(pl.BlockSpec((1, cc, tn), lambda e, s: (e, 0, nc(s))))
        shapes.append(jax.ShapeDtypeStruct((E, cc, D), F32))
        scratch.append(pltpu.VMEM((nj, cc, tf), BF16))
    args += [w_gate, w_up, w_down]
    specs += [pl.BlockSpec((1, 1, D, tf), lambda e, s: (layer, e, 0, jj(s))),
              pl.BlockSpec((1, 1, D, tf), lambda e, s: (layer, e, 0, jj(s))),
              pl.BlockSpec((1, 1, FF, tn), lambda e, s: (layer, e, 0, nc(s)))]
    res = pl.pallas_call(
        functools.partial(_moe_kernel, nj=nj, has_ctx=has_ctx),
        grid=(E, nj + nn),
        in_specs=specs,
        out_specs=outs,
        out_shape=shapes,
        scratch_shapes=scratch,
        compiler_params=_params("arbitrary", "arbitrary"),
    )(*args)
    return res if has_ctx else (res[0], None)


def _select_kernel(a_ref, idx_ref, *, cap):
    a = a_ref[0]
    R = a.shape[0]

    def count(mask):
        return jnp.sum(jnp.where(mask, 1.0, 0.0))

    def narrow(_, bounds):
        lo, hi = bounds
        q = 0.25 * (hi - lo)
        m1, m2, m3 = lo + q, lo + 2.0 * q, lo + 3.0 * q
        e1, e2, e3 = (count(a >= m) >= cap for m in (m1, m2, m3))
        lo = jnp.where(e3, m3, jnp.where(e2, m2, jnp.where(e1, m1, lo)))
        hi = jnp.where(e3, hi, jnp.where(e2, m3, jnp.where(e1, m2, m1)))
        return lo, hi

    lo, hi = lax.fori_loop(0, SELECT_QUARTERINGS, narrow, (jnp.float32(0.0), jnp.float32(2.0)))
    above = a >= hi
    equal = jnp.logical_and(a >= lo, a < hi)
    need = cap - count(above)

    li = lax.broadcasted_iota(jnp.int32, (LANES, LANES), 0)
    lj = lax.broadcasted_iota(jnp.int32, (LANES, LANES), 1)
    upper = jnp.where(li <= lj, 1.0, 0.0).astype(BF16)
    ones = jnp.ones((LANES, LANES), BF16)
    ri = lax.broadcasted_iota(jnp.int32, (R, R), 0)
    rj = lax.broadcasted_iota(jnp.int32, (R, R), 1)
    earlier_rows = jnp.where(rj < ri, 1.0, 0.0).astype(BF16)

    def rank(mask):
        x = jnp.where(mask, 1.0, 0.0)
        xb = x.astype(BF16)
        return _dot(xb, upper) - x + _dot(earlier_rows, _dot(xb, ones).astype(BF16))

    take = jnp.logical_or(above, jnp.logical_and(equal, rank(equal) < need))

    x = jnp.where(take, 1.0, 0.0)
    xb = x.astype(BF16)
    counts = rank(take) + x
    row_tot = _dot_nt(jnp.ones((SUBLANES, LANES), BF16), xb)[0:1]
    ui = lax.broadcasted_iota(jnp.int32, (R, R), 0)
    uj = lax.broadcasted_iota(jnp.int32, (R, R), 1)
    row_incl = _dot(row_tot.astype(BF16), jnp.where(ui <= uj, 1.0, 0.0).astype(BF16))
    row_excl = row_incl - row_tot
    slot = lax.broadcasted_iota(jnp.int32, (cap, R), 0).astype(F32)
    holds = jnp.logical_and(row_excl <= slot, slot < row_incl)
    hb = jnp.where(holds, 1.0, 0.0).astype(BF16)
    hi = jnp.floor(counts * (1.0 / LANES))
    lo = counts - hi * LANES
    row_counts = _dot(hb, hi.astype(BF16)) * LANES + _dot(hb, lo.astype(BF16))
    slot_l = lax.broadcasted_iota(jnp.int32, (cap, LANES), 0).astype(F32)
    in_row = jnp.sum(jnp.where(row_counts <= slot_l, 1.0, 0.0), axis=1, keepdims=True)
    rows_before = jnp.sum(jnp.where(row_incl <= slot, 1.0, 0.0), axis=1, keepdims=True)
    idx_ref[0] = (rows_before * LANES + in_row).astype(jnp.int32)


def _select(aff, cap):
    T = aff.shape[0]
    R = pl.cdiv(pl.cdiv(T, LANES), LANES) * LANES
    a = jnp.full((N_EXPERTS, R * LANES), -1.0, F32).at[:, :T].set(aff[:, :N_EXPERTS].T)
    idx = pl.pallas_call(
        functools.partial(_select_kernel, cap=cap),
        grid=(N_EXPERTS,),
        in_specs=[pl.BlockSpec((1, R, LANES), lambda e: (e, 0, 0))],
        out_specs=pl.BlockSpec((1, cap, 1), lambda e: (e, 0, 0)),
        out_shape=jax.ShapeDtypeStruct((N_EXPERTS, cap, 1), jnp.int32),
        compiler_params=_params("arbitrary"),
    )(a.reshape(N_EXPERTS, R, LANES))
    return idx.reshape(N_EXPERTS, cap)


def _route(aff, x, g2, shift, scale):
    cap = CAPACITY * aff.shape[0] // N_EXPERTS
    top_idx = _select(aff, cap)
    top_w = jnp.take_along_axis(aff[:, :N_EXPERTS].T, top_idx, axis=1)
    return _gather_rows(x, top_idx, g2, shift, scale), top_w[..., None], top_idx


def _reorder_w_in(w):
    D = w.shape[0]
    sizes = (GLA_QK, GLA_QK, GLA_V, GLA_V, GLA_GATE_RANK, GLA_GATE_RANK, NA_W, NA_W, NA_W, LRU_WIDTH, LRU_WIDTH)
    parts, o = [], 0
    for s in sizes:
        parts.append(w[:, o:o + s])
        o += s
    gq, gk, gv, gr, lf, lb, nq, nk, nv, lx, ly = parts
    pad = jnp.zeros((D, LANES - 2 * GLA_GATE_RANK), w.dtype)
    main = jnp.concatenate([nq, nk, nv, gv, gr, lx, ly, gq, gk], axis=1).astype(BF16)
    return main, jnp.concatenate([lf, lb, pad], axis=1).astype(BF16)


def kernel(x, c, ctx, c_ctx, norm1_g, norm2_g, w_mod, b_mod, w_in, gla_w_gate, gla_b_gate, gla_norm_g, na_q_norm_g, na_k_norm_g, na_rpb, lru_conv_w, lru_conv_b, lru_w_a, lru_b_a, lru_w_i, lru_b_i, lru_lambda, w_out, w_router, w_exp_gate, w_exp_up, w_exp_down):
    B, T, D = x.shape
    Lc = ctx.shape[1]
    depth = w_in.shape[0]
    assert B == 1 and T % (NA_ROWS_PER_STEP * GRID_W) == 0 and T // GRID_W >= NA_UNION_ROWS
    xs = x[0]
    cs = ctx[0]
    mod = _modulation(jnp.stack([c[0], c_ctx]), w_mod, b_mod)
    cos, sin = _rope_tables(T)
    tb_l, tb_c = min(512, T), min(512, Lc)
    tri = {(tb, rev): _tri_matrix(tb, rev) for tb in {tb_l, tb_c} for rev in (False, True)}
    row = lambda v: v.reshape(1, -1)

    for l in range(depth):
        ctx_out = l < depth - 1
        mx = [row(m) for m in jnp.split(mod[l, 0], 6)]
        mc = [row(m) for m in jnp.split(mod[l, 1], 6)]
        w_main, w_lr = _reorder_w_in(w_in[l])
        g1 = row(norm1_g[l])
        zx, zx_lr = _inproj(xs, g1, mx[0], mx[1], w_main, w_lr)
        zc, zc_lr = _inproj(cs, g1, mc[0], mc[1], w_main, w_lr)

        gla_dir = []
        for d in range(2):
            wg = jnp.zeros((LANES, GLA_QK), F32).at[d * GLA_GATE_RANK:(d + 1) * GLA_GATE_RANK].set(gla_w_gate[l, d])
            gla_dir.append((wg, row(gla_b_gate[l, d])))
        ng = row(gla_norm_g[l])
        zero_state = jnp.zeros((GLA_HEADS, GLA_DV, GLA_DK), F32)
        oc_f, sc_f = _gla_pass(zc, zc_lr, None, None, *gla_dir[0], tri[(tb_c, False)], zero_state, reverse=False)
        gla_c, sc_b = _gla_pass(zc, zc_lr, None, None, *gla_dir[1], tri[(tb_c, True)], zero_state, reverse=True,
                                oprev=oc_f, norm_g=ng)
        ol_f, _ = _gla_pass(zx, zx_lr, cos, sin, *gla_dir[0], tri[(tb_l, False)], sc_f, reverse=False)
        gla_x, _ = _gla_pass(zx, zx_lr, cos, sin, *gla_dir[1], tri[(tb_l, True)], sc_b, reverse=True,
                             oprev=ol_f, norm_g=ng)

        qg, kg = row(na_q_norm_g[l]), row(na_k_norm_g[l])
        ql, kl, vl = _na_prep(zx, qg, kg)
        qc, kc, vc = _na_prep(zc, qg, kg)
        na_x = _na_latent(ql, kl, vl, kc, vc, _na_bias(na_rpb[l]))

        lru_dir = []
        for d in range(2):
            lru_dir.append(dict(conv_w=lru_conv_w[l], conv_b=row(lru_conv_b[l]),
                                w_a=lru_w_a[l, d].astype(BF16), b_a=row(lru_b_a[l, d]),
                                w_i=lru_w_i[l, d].astype(BF16), b_i=row(lru_b_i[l, d]),
                                lam=row(lru_lambda[l, d])))
        zero_h = jnp.zeros((1, LRU_WIDTH), F32)
        hc_f, fin_f = _lru_pass(zc, lru_dir[0], zero_h, reverse=False)
        lru_c, fin_b = _lru_pass(zc, lru_dir[1], zero_h, reverse=True, hprev=hc_f)
        hl_f, _ = _lru_pass(zx, lru_dir[0], fin_f[0:1], reverse=False)
        lru_x, _ = _lru_pass(zx, lru_dir[1], fin_b[0:1], reverse=True, hprev=hl_f)

        wo = w_out[l].astype(BF16)
        wr = jnp.zeros((D, LANES), BF16).at[:, :N_EXPERTS].set(w_router[l].astype(BF16))
        g2 = row(norm2_g[l])
        x1, affx = _outproj(gla_x, na_x, lru_x, xs, wo, mx[2], g2, mx[3], mx[4], wr)
        xg, twx, idxx = _route(affx, x1, g2, mx[3], mx[4])
        if ctx_out:
            na_c = _na_context(qc, kc, vc)
            c1, affc = _outproj(gla_c, na_c, lru_c, cs, wo, mc[2], g2, mc[3], mc[4], wr)
            cg, twc, idxc = _route(affc, c1, g2, mc[3], mc[4])
            ye, yc = _moe(xg, twx, mx[5], l, w_exp_gate, w_exp_up, w_exp_down, ctx=(cg, twc, mc[5]))
            cs = _combine(c1, yc, idxc)
        else:
            ye, _ = _moe(xg, twx, mx[5], l, w_exp_gate, w_exp_up, w_exp_down)
        xs = _combine(x1, ye, idxx)
    return xs[None]
```

```python
import functools

import numpy as np
import jax
import jax.numpy as jnp
from jax import lax
from jax.experimental import pallas as pl
from jax.experimental.pallas import tpu as pltpu

F32 = jnp.float32
BF16 = jnp.bfloat16

GRID_W = 64
EPS = 1e-6
GLA_HEADS = 4
GLA_DK = 64
GLA_DV = 128
GLA_GATE_RANK = 16
GLA_TAU = 16.0
GLA_CHUNK = 64
ROPE_BASE = 10000.0
NA_HEADS = 8
NA_HD = 128
NA_KH = 8
NA_KW = 16
LRU_WIDTH = 512
LRU_BLOCKS = 4
LRU_CONV = 4
LRU_C = 8.0
N_EXPERTS = 16
CAPACITY = 2

GLA_QK = GLA_HEADS * GLA_DK
GLA_V = GLA_HEADS * GLA_DV
NA_W = NA_HEADS * NA_HD

LANES = 128
SUBLANES = 8
VMEM_LIMIT = 56 * 1024 * 1024
NEG = -1e30
ROW_TILE = 512
ISSUE_UNROLL = 8
SELECT_QUARTERINGS = 24
MOE_TILE = 256
MOE_ROW_CHUNK = 512
LRU_HALO = 16
NA_ROWS_PER_STEP = 8
NA_UNION_ROWS = 16
NA_Q_CHUNK = 256
NA_PAIR_OFFSET = NA_UNION_ROWS - NA_KH
NA_PAIR_ENTRIES = NA_UNION_ROWS + 2 * NA_KH - 2

Z_NAQ, Z_NAK, Z_NAV = 0, NA_W, 2 * NA_W
Z_GV = 3 * NA_W
Z_GR = Z_GV + GLA_V
Z_LX = Z_GR + GLA_V
Z_LY = Z_LX + LRU_WIDTH
Z_GQ = Z_LY + LRU_WIDTH
Z_GK = Z_GQ + GLA_QK
Z_COLS = Z_GK + GLA_QK
Z_CHUNK = 512


def _params(*sem):
    return pltpu.CompilerParams(dimension_semantics=sem, vmem_limit_bytes=VMEM_LIMIT)


def _dot(a, b):
    return jnp.dot(a, b, preferred_element_type=F32)


def _dot_nt(a, b):
    return lax.dot_general(a, b, (((1,), (1,)), ((), ())), preferred_element_type=F32)


def _dot_tn(a, b):
    return lax.dot_general(a, b, (((0,), (0,)), ((), ())), preferred_element_type=F32)


def _split2(a):
    hi = a.astype(BF16)
    lo = (a - hi.astype(F32)).astype(BF16)
    return hi, lo


def _dot_hi(a, b):
    ah, al = _split2(a)
    bh, bl = _split2(b)
    return _dot(ah, bh) + _dot(al, bh) + _dot(ah, bl)


def _rms(x):
    return x * lax.rsqrt(jnp.mean(x * x, axis=-1, keepdims=True) + EPS)


def _norm2(x, g_ref, sh_ref, sc_ref):
    return (_rms(x) * g_ref[...]) * (1.0 + sc_ref[...]) + sh_ref[...]


def _log_sigmoid(z):
    return jnp.minimum(z, 0.0) - jnp.log1p(jnp.exp(-jnp.abs(z)))


def _mod_kernel(cb_ref, w_ref, b_ref, o_ref):
    tn = w_ref.shape[2]
    s = [cb_ref[v] * jax.nn.sigmoid(cb_ref[v]) for v in range(2)]
    for j in range(tn // LANES):
        cs = slice(j * LANES, (j + 1) * LANES)
        wj = w_ref[0, :, cs]
        for v in range(2):
            o_ref[0, v:v + 1, cs] = jnp.sum(wj * s[v], axis=0, keepdims=True) + b_ref[0, :, cs]


def _modulation(c2, w_mod, b_mod):
    L, D, N6 = w_mod.shape
    tn = 1024
    cb = jnp.broadcast_to(c2[:, :, None], (2, D, LANES))
    return pl.pallas_call(
        _mod_kernel,
        grid=(L, N6 // tn),
        in_specs=[pl.BlockSpec((2, D, LANES), lambda l, j: (0, 0, 0)),
                  pl.BlockSpec((1, D, tn), lambda l, j: (l, 0, j)),
                  pl.BlockSpec((1, 1, tn), lambda l, j: (l, 0, j))],
        out_specs=pl.BlockSpec((1, 2, tn), lambda l, j: (l, 0, j)),
        out_shape=jax.ShapeDtypeStruct((L, 2, N6), F32),
        compiler_params=_params("arbitrary", "arbitrary"),
    )(cb, w_mod, b_mod.reshape(L, 1, N6))


def _inproj_kernel(x_ref, g_ref, sh_ref, sc_ref, w_ref, wlr_ref, o_ref, olr_ref):
    h = _norm2(x_ref[...], g_ref, sh_ref, sc_ref).astype(BF16)
    for n in range(w_ref.shape[1] // Z_CHUNK):
        cs = slice(n * Z_CHUNK, (n + 1) * Z_CHUNK)
        o_ref[:, cs] = _dot(h, w_ref[:, cs]).astype(BF16)
    olr_ref[...] = _dot(h, wlr_ref[...])


def _inproj(x, gain, shift, scale, w, wlr):
    T, D = x.shape
    NZ = w.shape[1]
    tm = min(512, T)
    vec = pl.BlockSpec((1, D), lambda i: (0, 0))
    resident = lambda a: pl.BlockSpec(a.shape, lambda i: (0, 0), pipeline_mode=pl.Buffered(1))
    return pl.pallas_call(
        _inproj_kernel,
        grid=(T // tm,),
        in_specs=[pl.BlockSpec((tm, D), lambda i: (i, 0)), vec, vec, vec, resident(w), resident(wlr)],
        out_specs=[pl.BlockSpec((tm, NZ), lambda i: (i, 0)), pl.BlockSpec((tm, LANES), lambda i: (i, 0))],
        out_shape=[jax.ShapeDtypeStruct((T, NZ), BF16), jax.ShapeDtypeStruct((T, LANES), F32)],
        compiler_params=_params("arbitrary"),
    )(x, gain, shift, scale, w, wlr)


def _gla_kernel(*refs, reverse, rope, final, tb):
    it = iter(refs)
    q_ref, k_ref, v_ref, lr_ref = next(it), next(it), next(it), next(it)
    cos_ref, sin_ref = (next(it), next(it)) if rope else (None, None)
    wg_ref, bg_ref, tri_ref, s0_ref = next(it), next(it), next(it), next(it)
    oprev_ref, r_ref, ng_ref = (next(it), next(it), next(it)) if final else (None, None, None)
    o_ref, sfin_ref, st_scr = next(it), next(it), next(it)

    @pl.when(pl.program_id(0) == 0)
    def _():
        st_scr[...] = s0_ref[...]

    q = q_ref[...].astype(F32)
    k = k_ref[...].astype(F32)
    if rope:
        lane = lax.broadcasted_iota(jnp.int32, q.shape, 1)
        first = (lane % GLA_DK) < (GLA_DK // 2)
        cos, sin = cos_ref[...], sin_ref[...]

        def rot(t):
            return jnp.where(first, pltpu.roll(t, GLA_QK - GLA_DK // 2, 1), pltpu.roll(t, GLA_DK // 2, 1))

        q = q * cos + rot(q) * sin
        k = k * cos + rot(k) * sin
    q = q * (GLA_DK ** -0.5)

    g = _log_sigmoid(_dot_hi(lr_ref[...], wg_ref[...]) + bg_ref[...]) * (1.0 / GLA_TAU)
    g_hi = g.astype(BF16)
    g_r1 = g - g_hi.astype(F32)
    g_mid = g_r1.astype(BF16)
    g_lo = (g_r1 - g_mid.astype(F32)).astype(BF16)
    tri = tri_ref[...]
    gc = _dot(tri, g_hi) + _dot(tri, g_mid) + _dot(tri, g_lo)

    C = GLA_CHUNK
    ri = lax.broadcasted_iota(jnp.int32, (C, C), 0)
    ci = lax.broadcasted_iota(jnp.int32, (C, C), 1)
    causal = (ci >= ri) if reverse else (ci <= ri)
    nchunk = tb // C
    for c in (range(nchunk - 1, -1, -1) if reverse else range(nchunk)):
        sl = slice(c * C, (c + 1) * C)
        gcc = gc[sl]
        gt = gcc[0:1] if reverse else gcc[C - 1:C]
        qe = q[sl] * jnp.exp(gcc)
        ke = k[sl] * jnp.exp(-gcc)
        kd = k[sl] * jnp.exp(gt - gcc)
        dec = jnp.exp(gt)
        for h in range(GLA_HEADS):
            hs = slice(h * GLA_DK, (h + 1) * GLA_DK)
            vs = slice(h * GLA_DV, (h + 1) * GLA_DV)
            qh = qe[:, hs].astype(BF16)
            vh = v_ref[sl, vs]
            att = jnp.where(causal, _dot_nt(qh, ke[:, hs].astype(BF16)), 0.0)
            st = st_scr[h]
            o = _dot(att.astype(BF16), vh) + _dot_nt(qh, st.astype(BF16))
            st_scr[h] = st * dec[:, hs] + _dot_tn(vh, kd[:, hs].astype(BF16))
            if final:
                y = _rms(oprev_ref[sl, vs] + o) * ng_ref[...]
                rg = r_ref[sl, vs].astype(F32)
                o_ref[sl, vs] = (y * (rg * jax.nn.sigmoid(rg))).astype(o_ref.dtype)
            else:
                o_ref[sl, vs] = o
    sfin_ref[...] = st_scr[...]


def _gla_pass(z, zlr, cos, sin, wg, bg, tri, s0, *, reverse, oprev=None, norm_g=None):
    T = z.shape[0]
    tb = tri.shape[0]
    nb = T // tb
    rope = cos is not None
    final = oprev is not None
    blk = (lambda i: nb - 1 - i) if reverse else (lambda i: i)

    def col(width, start):
        return pl.BlockSpec((tb, width), lambda i: (blk(i), start // width))

    const2 = lambda shape: pl.BlockSpec(shape, lambda i: (0, 0))
    state = pl.BlockSpec((GLA_HEADS, GLA_DV, GLA_DK), lambda i: (0, 0, 0))
    args = [z, z, z, zlr]
    specs = [col(GLA_QK, Z_GQ), col(GLA_QK, Z_GK), col(GLA_V, Z_GV), col(LANES, 0)]
    if rope:
        args += [cos, sin]
        specs += [col(GLA_QK, 0), col(GLA_QK, 0)]
    args += [wg, bg, tri, s0]
    specs += [const2((LANES, GLA_QK)), const2((1, GLA_QK)), const2((tb, tb)), state]
    if final:
        args += [oprev, z, norm_g]
        specs += [col(GLA_V, 0), col(GLA_V, Z_GR), const2((1, GLA_DV))]
    return pl.pallas_call(
        functools.partial(_gla_kernel, reverse=reverse, rope=rope, final=final, tb=tb),
        grid=(nb,),
        in_specs=specs,
        out_specs=[col(GLA_V, 0), state],
        out_shape=[jax.ShapeDtypeStruct((T, GLA_V), BF16 if final else F32),
                   jax.ShapeDtypeStruct((GLA_HEADS, GLA_DV, GLA_DK), F32)],
        scratch_shapes=[pltpu.VMEM((GLA_HEADS, GLA_DV, GLA_DK), F32)],
        compiler_params=_params("arbitrary"),
    )(*args)


def _tri_matrix(tb, reverse):
    i = np.arange(tb)
    same = (i[:, None] // GLA_CHUNK) == (i[None, :] // GLA_CHUNK)
    order = (i[None, :] >= i[:, None]) if reverse else (i[None, :] <= i[:, None])
    return jnp.asarray(same & order, BF16)


def _rope_tables(T):
    pos = jnp.arange(T)
    row = (pos // GRID_W).astype(F32)
    col = (pos % GRID_W).astype(F32)
    nf = GLA_DK // 4
    inv = ROPE_BASE ** (-jnp.arange(nf, dtype=F32) / nf)
    ang = jnp.concatenate([row[:, None] * inv, col[:, None] * inv], axis=-1)
    cos, sin = jnp.cos(ang), jnp.sin(ang)
    cos_full = jnp.tile(jnp.concatenate([cos, cos], axis=-1), (1, GLA_HEADS))
    sin_full = jnp.tile(jnp.concatenate([-sin, sin], axis=-1), (1, GLA_HEADS))
    return cos_full, sin_full


def _naprep_kernel(q_ref, k_ref, v_ref, qg_ref, kg_ref, qo_ref, ko_ref, vo_ref):
    for h in range(NA_HEADS):
        hs = slice(h * NA_HD, (h + 1) * NA_HD)
        qo_ref[h] = (_rms(q_ref[:, hs].astype(F32)) * qg_ref[...] * (NA_HD ** -0.5)).astype(BF16)
        ko_ref[h] = (_rms(k_ref[:, hs].astype(F32)) * kg_ref[...]).astype(BF16)
        vo_ref[h] = v_ref[:, hs]


def _na_prep(z, qg, kg):
    T = z.shape[0]
    tm = min(512, T)
    col = lambda c: pl.BlockSpec((tm, NA_W), lambda i: (i, c))
    vec = pl.BlockSpec((1, NA_HD), lambda i: (0, 0))
    out = pl.BlockSpec((NA_HEADS, tm, NA_HD), lambda i: (0, i, 0))
    shp = jax.ShapeDtypeStruct((NA_HEADS, T, NA_HD), BF16)
    return pl.pallas_call(
        _naprep_kernel,
        grid=(T // tm,),
        in_specs=[col(Z_NAQ // NA_W), col(Z_NAK // NA_W), col(Z_NAV // NA_W), vec, vec],
        out_specs=[out, out, out],
        out_shape=[shp, shp, shp],
        compiler_params=_params("arbitrary"),
    )(z, z, z, qg, kg)


def _na_kernel(q_ref, k_ref, v_ref, kc_ref, vc_ref, bias_ref, o_ref, *, n_rows):
    i = pl.program_id(1)
    W = GRID_W
    R = NA_ROWS_PER_STEP
    first_row = jnp.clip(i * R - NA_KH // 2, 0, n_rows - NA_UNION_ROWS)
    start = pl.multiple_of(first_row * W, W)
    kw = k_ref[0, pl.ds(start, NA_UNION_ROWS * W), :]
    vw = v_ref[0, pl.ds(start, NA_UNION_ROWS * W), :]
    kc = kc_ref[0]
    vc = vc_ref[0]
    upper = lax.broadcasted_iota(jnp.int32, (1, 2 * W), 1) // W

    def bias_rows(rr):
        r = i * R + rr
        lo = jnp.clip(r - NA_KH // 2, 0, n_rows - NA_KH) - r + (NA_KH - 1)
        blocks = []
        for m in range(NA_UNION_ROWS // 2):
            a_even = first_row + 2 * m - r + (NA_KH - 1)
            a = a_even + upper
            ok = jnp.logical_and(a >= lo, a < lo + NA_KH)
            blocks.append(jnp.where(ok, bias_ref[0, a_even + NA_PAIR_OFFSET], NEG))
        return jnp.concatenate(blocks, axis=1)

    nq = q_ref.shape[1]
    step = min(NA_Q_CHUNK, nq)
    for c in range(nq // step):
        qs = slice(c * step, (c + 1) * step)
        q = q_ref[0, qs, :]
        bias = jnp.concatenate([bias_rows(rr) for rr in range(c * step // W, (c + 1) * step // W)], axis=0)
        s_loc = _dot_nt(q, kw) + bias
        s_ctx = _dot_nt(q, kc)
        m = jnp.maximum(jnp.max(s_loc, axis=-1, keepdims=True), jnp.max(s_ctx, axis=-1, keepdims=True))
        p_loc = jnp.exp(s_loc - m)
        p_ctx = jnp.exp(s_ctx - m)
        den = jnp.sum(p_loc, axis=-1, keepdims=True) + jnp.sum(p_ctx, axis=-1, keepdims=True)
        o = _dot(p_loc.astype(BF16), vw) + _dot(p_ctx.astype(BF16), vc)
        o_ref[qs, :] = (o / den).astype(BF16)


def _na_latent(q, k, v, kc, vc, bias):
    H, T, hd = q.shape
    Lc = kc.shape[1]
    n_rows = T // GRID_W
    R = NA_ROWS_PER_STEP
    nblk = n_rows // R
    whole = lambda n: pl.BlockSpec((1, n, hd), lambda h, i: (h, 0, 0))
    return pl.pallas_call(
        functools.partial(_na_kernel, n_rows=n_rows),
        grid=(H, nblk),
        in_specs=[pl.BlockSpec((1, R * GRID_W, hd), lambda h, i: (h, i, 0)),
                  whole(T), whole(T), whole(Lc), whole(Lc),
                  pl.BlockSpec((1,) + bias.shape[1:], lambda h, i: (h, 0, 0, 0))],
        out_specs=pl.BlockSpec((R * GRID_W, hd), lambda h, i: (i, h)),
        out_shape=jax.ShapeDtypeStruct((T, H * hd), BF16),
        compiler_params=_params("arbitrary", "arbitrary"),
    )(q, k, v, kc, vc, bias)


def _ctxattn_kernel(q_ref, k_ref, v_ref, o_ref):
    s = _dot_nt(q_ref[0], k_ref[0])
    p = jnp.exp(s - jnp.max(s, axis=-1, keepdims=True))
    o = _dot(p.astype(BF16), v_ref[0]) / jnp.sum(p, axis=-1, keepdims=True)
    o_ref[...] = o.astype(BF16)


def _na_context(q, k, v):
    H, Lc, hd = q.shape
    spec = pl.BlockSpec((1, Lc, hd), lambda h: (h, 0, 0))
    return pl.pallas_call(
        _ctxattn_kernel,
        grid=(H,),
        in_specs=[spec, spec, spec],
        out_specs=pl.BlockSpec((Lc, hd), lambda h: (0, h)),
        out_shape=jax.ShapeDtypeStruct((Lc, H * hd), BF16),
        compiler_params=_params("arbitrary"),
    )(q, k, v)


def _na_bias(rpb):
    W = GRID_W
    cols = np.arange(W)
    col_start = np.clip(cols - NA_KW // 2, 0, W - NA_KW)
    col_ok = (cols[None, :] >= col_start[:, None]) & (cols[None, :] < col_start[:, None] + NA_KW)
    dc = np.clip(cols[None, :] - cols[:, None] + (NA_KW - 1), 0, 2 * NA_KW - 2)
    tab = jnp.where(col_ok[None, None], rpb[:, :, dc], NEG).astype(F32)
    n_tab = 2 * NA_KH - 1
    pad_lo = jnp.full((rpb.shape[0], NA_PAIR_OFFSET, W, W), NEG, F32)
    pad_hi = jnp.full((rpb.shape[0], NA_PAIR_ENTRIES + 1 - NA_PAIR_OFFSET - n_tab, W, W), NEG, F32)
    ext = jnp.concatenate([pad_lo, tab, pad_hi], axis=1)
    return jnp.concatenate([ext[:, :-1], ext[:, 1:]], axis=-1)


def _lru_kernel(*refs, reverse, final, tb, nblk):
    it = iter(refs)
    x_ref, xp_ref, xn_ref = next(it), next(it), next(it)
    cw_ref, cb_ref, wa_ref, ba_ref, wi_ref, bi_ref, lam_ref, h0_ref = (next(it) for _ in range(8))
    hprev_ref, y_ref = (next(it), next(it)) if final else (None, None)
    o_ref, hfin_ref = next(it), next(it)
    xx_scr, a_scr, b_scr, carry_scr = next(it), next(it), next(it), next(it)

    i = pl.program_id(0)
    blk = (nblk - 1 - i) if reverse else i
    S = SUBLANES
    Wd = LRU_WIDTH

    @pl.when(i == 0)
    def _():
        carry_scr[...] = jnp.broadcast_to(h0_ref[...], (S, Wd))

    H = LRU_HALO
    xx_scr[0:H] = jnp.where(blk > 0, xp_ref[...].astype(F32), 0.0)
    xx_scr[H:H + tb] = x_ref[...].astype(F32)
    xx_scr[H + tb:2 * H + tb] = jnp.where(blk < nblk - 1, xn_ref[...].astype(F32), 0.0)
    xc = cb_ref[...]
    for j in range(LRU_CONV):
        xc = xc + cw_ref[j:j + 1] * xx_scr[H - 1 + j:H - 1 + j + tb]

    bw = Wd // LRU_BLOCKS
    ra, ri = [], []
    for n in range(LRU_BLOCKS):
        xb = xc[:, n * bw:(n + 1) * bw].astype(BF16)
        ra.append(_dot(xb, wa_ref[n]))
        ri.append(_dot(xb, wi_ref[n]))
    rg = jax.nn.sigmoid(jnp.concatenate(ra, axis=1) + ba_ref[...])
    ig = jax.nn.sigmoid(jnp.concatenate(ri, axis=1) + bi_ref[...])
    log_a = LRU_C * rg * _log_sigmoid(lam_ref[...])
    a = jnp.exp(log_a)
    a_scr[...] = a
    b_scr[...] = jnp.sqrt(-jnp.tanh(log_a) * (a * a + 1.0)) * (ig * xc)

    row = lax.broadcasted_iota(jnp.int32, (S, Wd), 0)
    ng = tb // S

    def body(gi, carry):
        g = (ng - 1 - gi) if reverse else gi
        off = pl.multiple_of(g * S, S)
        a = a_scr[pl.ds(off, S), :]
        b = b_scr[pl.ds(off, S), :]
        for s in (1, 2, 4):
            sh = (S - s) if reverse else s
            keep = (row < S - s) if reverse else (row >= s)
            b = jnp.where(keep, a * pltpu.roll(b, sh, 0) + b, b)
            a = jnp.where(keep, a * pltpu.roll(a, sh, 0), a)
        h = b + a * carry
        b_scr[pl.ds(off, S), :] = h
        return jnp.broadcast_to(h[0:1] if reverse else h[S - 1:S], (S, Wd))

    carry = lax.fori_loop(0, ng, body, carry_scr[...])
    carry_scr[...] = carry
    hfin_ref[...] = carry
    if final:
        o_ref[...] = ((hprev_ref[...] + b_scr[...]) * jax.nn.gelu(y_ref[...].astype(F32))).astype(o_ref.dtype)
    else:
        o_ref[...] = b_scr[...]


def _lru_pass(z, p, h0, *, reverse, hprev=None):
    T = z.shape[0]
    tb = min(512, T)
    nb = T // tb
    S = SUBLANES
    Wd = LRU_WIDTH
    final = hprev is not None
    blk = (lambda i: nb - 1 - i) if reverse else (lambda i: i)
    H = LRU_HALO
    gpb = tb // H
    xcol = Z_LX // Wd
    row = pl.BlockSpec((1, Wd), lambda i: (0, 0))
    wspec = pl.BlockSpec((LRU_BLOCKS, Wd // LRU_BLOCKS, Wd // LRU_BLOCKS), lambda i: (0, 0, 0))
    args = [z, z, z, p["conv_w"], p["conv_b"], p["w_a"], p["b_a"], p["w_i"], p["b_i"], p["lam"], h0]
    specs = [pl.BlockSpec((tb, Wd), lambda i: (blk(i), xcol)),
             pl.BlockSpec((H, Wd), lambda i: (jnp.maximum(blk(i) * gpb - 1, 0), xcol)),
             pl.BlockSpec((H, Wd), lambda i: (jnp.minimum((blk(i) + 1) * gpb, T // H - 1), xcol)),
             pl.BlockSpec((LRU_CONV, Wd), lambda i: (0, 0)), row, wspec, row, wspec, row, row, row]
    if final:
        args += [hprev, z]
        specs += [pl.BlockSpec((tb, Wd), lambda i: (blk(i), 0)),
                  pl.BlockSpec((tb, Wd), lambda i: (blk(i), Z_LY // Wd))]
    return pl.pallas_call(
        functools.partial(_lru_kernel, reverse=reverse, final=final, tb=tb, nblk=nb),
        grid=(nb,),
        in_specs=specs,
        out_specs=[pl.BlockSpec((tb, Wd), lambda i: (blk(i), 0)), pl.BlockSpec((S, Wd), lambda i: (0, 0))],
        out_shape=[jax.ShapeDtypeStruct((T, Wd), BF16 if final else F32),
                   jax.ShapeDtypeStruct((S, Wd), F32)],
        scratch_shapes=[pltpu.VMEM((tb + 2 * H, Wd), F32), pltpu.VMEM((tb, Wd), F32),
                        pltpu.VMEM((tb, Wd), F32), pltpu.VMEM((S, Wd), F32)],
        compiler_params=_params("arbitrary"),
    )(*args)


def _outproj_kernel(gla_ref, na_ref, lru_ref, x_ref, wo_ref, gate_ref, g2_ref, sh_ref, sc_ref,
                    wr_ref, xo_ref, aff_ref, mix_scr):
    mix_scr[:, 0:GLA_V] = gla_ref[...]
    mix_scr[:, GLA_V:GLA_V + NA_W] = na_ref[...]
    mix_scr[:, GLA_V + NA_W:] = lru_ref[...]
    xn = x_ref[...] + gate_ref[...] * _dot(mix_scr[...], wo_ref[...])
    xo_ref[...] = xn
    h = _norm2(xn, g2_ref, sh_ref, sc_ref)
    logits = _dot(h.astype(BF16), wr_ref[...])
    lane = lax.broadcasted_iota(jnp.int32, logits.shape, 1)
    logits = jnp.where(lane < N_EXPERTS, logits, NEG)
    e = jnp.exp(logits - jnp.max(logits, axis=-1, keepdims=True))
    aff_ref[...] = e / jnp.sum(e, axis=-1, keepdims=True)


def _outproj(gla, na, lru, x, wo, gate, g2, shift, scale, wr):
    T, D = x.shape
    tm = min(512, T)
    rows = lambda w: pl.BlockSpec((tm, w), lambda i: (i, 0))
    full = lambda a: pl.BlockSpec(a.shape, lambda i: (0, 0), pipeline_mode=pl.Buffered(1))
    vec = pl.BlockSpec((1, D), lambda i: (0, 0))
    return pl.pallas_call(
        _outproj_kernel,
        grid=(T // tm,),
        in_specs=[rows(GLA_V), rows(NA_W), rows(LRU_WIDTH), rows(D), full(wo), vec, vec, vec, vec, full(wr)],
        out_specs=[rows(D), rows(LANES)],
        out_shape=[jax.ShapeDtypeStruct((T, D), F32), jax.ShapeDtypeStruct((T, LANES), F32)],
        scratch_shapes=[pltpu.VMEM((tm, wo.shape[0]), BF16)],
        compiler_params=_params("arbitrary"),
    )(gla, na, lru, x, wo, gate, g2, shift, scale, wr)


def _gather_kernel(idx_ref, h_hbm, g2_ref, sh_ref, sc_ref, o_ref, buf, sem, *, rt, nt, ne):
    e = pl.program_id(0)
    t = pl.program_id(1)

    step = e * nt + t
    slot = step % 2

    def start_rows(first, s):
        def issue(p, carry):
            for u in range(2):
                r = 2 * p + u
                tok = idx_ref[first + r]
                pltpu.make_async_copy(h_hbm.at[pl.ds(tok, 1)], buf.at[s, pl.ds(r, 1)],
                                      sem.at[s]).start(priority=u)
            return carry

        lax.fori_loop(0, rt // 2, issue, 0, unroll=ISSUE_UNROLL // 2)

    @pl.when(step == 0)
    def _():
        start_rows(0, 0)

    @pl.when(step + 1 < ne * nt)
    def _():
        start_rows((step + 1) * rt, 1 - slot)

    pltpu.make_async_copy(h_hbm.at[pl.ds(0, rt)], buf.at[slot], sem.at[slot]).wait()
    o_ref[0] = _norm2(buf[slot], g2_ref, sh_ref, sc_ref).astype(BF16)


def _gather_rows(x, idx, g2, shift, scale):
    T, D = x.shape
    E, cap = idx.shape
    rt = min(ROW_TILE, cap)
    nt = cap // rt
    vec = pl.BlockSpec((1, D), lambda e, t, idx: (0, 0))
    return pl.pallas_call(
        functools.partial(_gather_kernel, rt=rt, nt=nt, ne=E),
        grid_spec=pltpu.PrefetchScalarGridSpec(
            num_scalar_prefetch=1,
            grid=(E, nt),
            in_specs=[pl.BlockSpec(memory_space=pl.ANY), vec, vec, vec],
            out_specs=pl.BlockSpec((1, rt, D), lambda e, t, idx: (e, t, 0)),
            scratch_shapes=[pltpu.VMEM((2, rt, D), F32), pltpu.SemaphoreType.DMA((2,))]),
        out_shape=jax.ShapeDtypeStruct((E, cap, D), BF16),
        compiler_params=_params("arbitrary", "arbitrary"),
    )(idx.reshape(-1), x, g2, shift, scale)


def _combine_kernel(idx_ref, ye_ref, x_in, x_out, buf, gsem, ssem, *, rt, nt):
    del x_in
    e = pl.program_id(0)
    t = pl.program_id(1)
    slot = t % 2

    def start_rows(tile, s, gather):
        first = (e * nt + tile) * rt

        def body(p, carry):
            for u in range(2):
                r = 2 * p + u
                tok = idx_ref[first + r]
                hbm_row = x_out.at[pl.ds(tok, 1)]
                vmem_row = buf.at[s, pl.ds(r, 1)]
                if gather:
                    pltpu.make_async_copy(hbm_row, vmem_row, gsem.at[s]).start(priority=u)
                else:
                    pltpu.make_async_copy(vmem_row, hbm_row, ssem.at[s]).start(priority=u)
            return carry

        lax.fori_loop(0, rt // 2, body, 0, unroll=ISSUE_UNROLL // 2)

    def wait_rows(s, gather):
        if gather:
            pltpu.make_async_copy(x_out.at[pl.ds(0, rt)], buf.at[s], gsem.at[s]).wait()
        else:
            pltpu.make_async_copy(buf.at[s], x_out.at[pl.ds(0, rt)], ssem.at[s]).wait()

    @pl.when(t == 0)
    def _():
        start_rows(0, 0, True)

    wait_rows(slot, True)

    @pl.when(t + 1 < nt)
    def _():
        @pl.when(t >= 1)
        def _():
            wait_rows(1 - slot, False)

        start_rows(t + 1, 1 - slot, True)

    buf[slot] = buf[slot] + ye_ref[0]
    start_rows(t, slot, False)

    @pl.when(t == nt - 1)
    def _():
        wait_rows(slot, False)
        if nt > 1:
            wait_rows(1 - slot, False)


def _combine(x, ye, idx):
    E, cap, D = ye.shape
    rt = min(ROW_TILE, cap)
    nt = cap // rt
    return pl.pallas_call(
        functools.partial(_combine_kernel, rt=rt, nt=nt),
        grid_spec=pltpu.PrefetchScalarGridSpec(
            num_scalar_prefetch=1,
            grid=(E, nt),
            in_specs=[pl.BlockSpec((1, rt, D), lambda e, t, idx: (e, t, 0)),
                      pl.BlockSpec(memory_space=pl.ANY)],
            out_specs=pl.BlockSpec(memory_space=pl.ANY),
            scratch_shapes=[pltpu.VMEM((2, rt, D), F32), pltpu.SemaphoreType.DMA((2,)),
                            pltpu.SemaphoreType.DMA((2,))]),
        out_shape=jax.ShapeDtypeStruct(x.shape, F32),
        input_output_aliases={2: 0},
        compiler_params=_params("arbitrary", "arbitrary"),
    )(idx.reshape(-1), ye, x)


def _moe_kernel(*refs, nj, has_ctx):
    it = iter(refs)
    xs_ref, tw_ref, gate_ref = next(it), next(it), next(it)
    xc_ref, twc_ref, gatec_ref = (next(it), next(it), next(it)) if has_ctx else (None, None, None)
    wg_ref, wu_ref, wd_ref = next(it), next(it), next(it)
    ye_ref = next(it)
    yc_ref = next(it) if has_ctx else None
    hid_scr = next(it)
    hidc_scr = next(it) if has_ctx else None
    s = pl.program_id(1)
    tf = wg_ref.shape[3]

    def row_chunks(n_rows):
        step = min(MOE_ROW_CHUNK, n_rows)
        return [slice(m * step, (m + 1) * step) for m in range(n_rows // step)]

    def up(x_ref, h_scr):
        wg = wg_ref[0, 0].astype(BF16)
        wu = wu_ref[0, 0].astype(BF16)
        for rs in row_chunks(x_ref.shape[1]):
            x = x_ref[0, rs]
            a = _dot(x, wg)
            h_scr[s, rs] = ((a * jax.nn.sigmoid(a)) * _dot(x, wu)).astype(BF16)

    def down(h_scr, y_ref, w_ref, g_ref):
        wd = wd_ref[0, 0].astype(BF16)
        for rs in row_chunks(y_ref.shape[1]):
            acc = _dot(h_scr[0, rs], wd[0:tf])
            for j in range(1, nj):
                acc = acc + _dot(h_scr[j, rs], wd[j * tf:(j + 1) * tf])
            y_ref[0, rs] = acc * w_ref[0, rs] * g_ref[...]

    @pl.when(s < nj)
    def _():
        up(xs_ref, hid_scr)
        if has_ctx:
            up(xc_ref, hidc_scr)

    @pl.when(s >= nj)
    def _():
        down(hid_scr, ye_ref, tw_ref, gate_ref)
        if has_ctx:
            down(hidc_scr, yc_ref, twc_ref, gatec_ref)


def _moe(xs, tw, gate, layer, w_gate, w_up, w_down, ctx=None):
    E, cap, D = xs.shape
    FF = w_gate.shape[3]
    tf = tn = MOE_TILE
    nj, nn = FF // tf, D // tn
    has_ctx = ctx is not None
    jj = lambda s: jnp.minimum(s, nj - 1)
    nc = lambda s: jnp.maximum(s - nj, 0)
    gvec = pl.BlockSpec((1, tn), lambda e, s: (0, nc(s)))
    args = [xs, tw, gate]
    specs = [pl.BlockSpec((1, cap, D), lambda e, s: (e, 0, 0)),
             pl.BlockSpec((1, cap, 1), lambda e, s: (e, 0, 0)), gvec]
    outs = [pl.BlockSpec((1, cap, tn), lambda e, s: (e, 0, nc(s)))]
    shapes = [jax.ShapeDtypeStruct((E, cap, D), F32)]
    scratch = [pltpu.VMEM((nj, cap, tf), BF16)]
    if has_ctx:
        xc, twc, gatec = ctx
        cc = xc.shape[1]
        args += [xc, twc, gatec]
        specs += [pl.BlockSpec((1, cc, D), lambda e, s: (e, 0, 0)),
                  pl.BlockSpec((1, cc, 1), lambda e, s: (e, 0, 0)), gvec]
        outs.append(pl.BlockSpec((1, cc, tn), lambda e, s: (e, 0, nc(s))))
        shapes.append(jax.ShapeDtypeStruct((E, cc, D), F32))
        scratch.append(pltpu.VMEM((nj, cc, tf), BF16))
    args += [w_gate, w_up, w_down]
    specs += [pl.BlockSpec((1, 1, D, tf), lambda e, s: (layer, e, 0, jj(s))),
              pl.BlockSpec((1, 1, D, tf), lambda e, s: (layer, e, 0, jj(s))),
              pl.BlockSpec((1, 1, FF, tn), lambda e, s: (layer, e, 0, nc(s)))]
    res = pl.pallas_call(
        functools.partial(_moe_kernel, nj=nj, has_ctx=has_ctx),
        grid=(E, nj + nn),
        in_specs=specs,
        out_specs=outs,
        out_shape=shapes,
        scratch_shapes=scratch,
        compiler_params=_params("arbitrary", "arbitrary"),
    )(*args)
    return res if has_ctx else (res[0], None)


def _select_kernel(a_ref, idx_ref, *, cap):
    a = a_ref[0]
    R = a.shape[0]

    def count(mask):
        return jnp.sum(jnp.where(mask, 1.0, 0.0))

    def narrow(_, bounds):
        lo, hi = bounds
        q = 0.25 * (hi - lo)
        m1, m2, m3 = lo + q, lo + 2.0 * q, lo + 3.0 * q
        e1, e2, e3 = (count(a >= m) >= cap for m in (m1, m2, m3))
        lo = jnp.where(e3, m3, jnp.where(e2, m2, jnp.where(e1, m1, lo)))
        hi = jnp.where(e3, hi, jnp.where(e2, m3, jnp.where(e1, m2, m1)))
        return lo, hi

    lo, hi = lax.fori_loop(0, SELECT_QUARTERINGS, narrow, (jnp.float32(0.0), jnp.float32(2.0)))
    above = a >= hi
    equal = jnp.logical_and(a >= lo, a < hi)
    need = cap - count(above)

    li = lax.broadcasted_iota(jnp.int32, (LANES, LANES), 0)
    lj = lax.broadcasted_iota(jnp.int32, (LANES, LANES), 1)
    upper = jnp.where(li <= lj, 1.0, 0.0).astype(BF16)
    ones = jnp.ones((LANES, LANES), BF16)
    ri = lax.broadcasted_iota(jnp.int32, (R, R), 0)
    rj = lax.broadcasted_iota(jnp.int32, (R, R), 1)
    earlier_rows = jnp.where(rj < ri, 1.0, 0.0).astype(BF16)

    def rank(mask):
        x = jnp.where(mask, 1.0, 0.0)
        xb = x.astype(BF16)
        return _dot(xb, upper) - x + _dot(earlier_rows, _dot(xb, ones).astype(BF16))

    take = jnp.logical_or(above, jnp.logical_and(equal, rank(equal) < need))

    x = jnp.where(take, 1.0, 0.0)
    xb = x.astype(BF16)
    counts = rank(take) + x
    row_tot = _dot_nt(jnp.ones((SUBLANES, LANES), BF16), xb)[0:1]
    ui = lax.broadcasted_iota(jnp.int32, (R, R), 0)
    uj = lax.broadcasted_iota(jnp.int32, (R, R), 1)
    row_incl = _dot(row_tot.astype(BF16), jnp.where(ui <= uj, 1.0, 0.0).astype(BF16))
    row_excl = row_incl - row_tot
    slot = lax.broadcasted_iota(jnp.int32, (cap, R), 0).astype(F32)
    holds = jnp.logical_and(row_excl <= slot, slot < row_incl)
    hb = jnp.where(holds, 1.0, 0.0).astype(BF16)
    hi = jnp.floor(counts * (1.0 / LANES))
    lo = counts - hi * LANES
    row_counts = _dot(hb, hi.astype(BF16)) * LANES + _dot(hb, lo.astype(BF16))
    slot_l = lax.broadcasted_iota(jnp.int32, (cap, LANES), 0).astype(F32)
    in_row = jnp.sum(jnp.where(row_counts <= slot_l, 1.0, 0.0), axis=1, keepdims=True)
    rows_before = jnp.sum(jnp.where(row_incl <= slot, 1.0, 0.0), axis=1, keepdims=True)
    idx_ref[0] = (rows_before * LANES + in_row).astype(jnp.int32)


def _select(aff, cap):
    T = aff.shape[0]
    R = pl.cdiv(pl.cdiv(T, LANES), LANES) * LANES
    a = jnp.full((N_EXPERTS, R * LANES), -1.0, F32).at[:, :T].set(aff[:, :N_EXPERTS].T)
    idx = pl.pallas_call(
        functools.partial(_select_kernel, cap=cap),
        grid=(N_EXPERTS,),
        in_specs=[pl.BlockSpec((1, R, LANES), lambda e: (e, 0, 0))],
        out_specs=pl.BlockSpec((1, cap, 1), lambda e: (e, 0, 0)),
        out_shape=jax.ShapeDtypeStruct((N_EXPERTS, cap, 1), jnp.int32),
        compiler_params=_params("arbitrary"),
    )(a.reshape(N_EXPERTS, R, LANES))
    return idx.reshape(N_EXPERTS, cap)


def _route(aff, x, g2, shift, scale):
    cap = CAPACITY * aff.shape[0] // N_EXPERTS
    top_idx = _select(aff, cap)
    top_w = jnp.take_along_axis(aff[:, :N_EXPERTS].T, top_idx, axis=1)
    return _gather_rows(x, top_idx, g2, shift, scale), top_w[..., None], top_idx


def _reorder_w_in(w):
    D = w.shape[0]
    sizes = (GLA_QK, GLA_QK, GLA_V, GLA_V, GLA_GATE_RANK, GLA_GATE_RANK, NA_W, NA_W, NA_W, LRU_WIDTH, LRU_WIDTH)
    parts, o = [], 0
    for s in sizes:
        parts.append(w[:, o:o + s])
        o += s
    gq, gk, gv, gr, lf, lb, nq, nk, nv, lx, ly = parts
    pad = jnp.zeros((D, LANES - 2 * GLA_GATE_RANK), w.dtype)
    main = jnp.concatenate([nq, nk, nv, gv, gr, lx, ly, gq, gk], axis=1).astype(BF16)
    return main, jnp.concatenate([lf, lb, pad], axis=1).astype(BF16)


def kernel(x, c, ctx, c_ctx, norm1_g, norm2_g, w_mod, b_mod, w_in, gla_w_gate, gla_b_gate, gla_norm_g, na_q_norm_g, na_k_norm_g, na_rpb, lru_conv_w, lru_conv_b, lru_w_a, lru_b_a, lru_w_i, lru_b_i, lru_lambda, w_out, w_router, w_exp_gate, w_exp_up, w_exp_down):
    B, T, D = x.shape
    Lc = ctx.shape[1]
    depth = w_in.shape[0]
    assert B == 1 and T % (NA_ROWS_PER_STEP * GRID_W) == 0 and T // GRID_W >= NA_UNION_ROWS
    xs = x[0]
    cs = ctx[0]
    mod = _modulation(jnp.stack([c[0], c_ctx]), w_mod, b_mod)
    cos, sin = _rope_tables(T)
    tb_l, tb_c = min(512, T), min(512, Lc)
    tri = {(tb, rev): _tri_matrix(tb, rev) for tb in {tb_l, tb_c} for rev in (False, True)}
    row = lambda v: v.reshape(1, -1)

    for l in range(depth):
        ctx_out = l < depth - 1
        mx = [row(m) for m in jnp.split(mod[l, 0], 6)]
        mc = [row(m) for m in jnp.split(mod[l, 1], 6)]
        w_main, w_lr = _reorder_w_in(w_in[l])
        g1 = row(norm1_g[l])
        zx, zx_lr = _inproj(xs, g1, mx[0], mx[1], w_main, w_lr)
        zc, zc_lr = _inproj(cs, g1, mc[0], mc[1], w_main, w_lr)

        gla_dir = []
        for d in range(2):
            wg = jnp.zeros((LANES, GLA_QK), F32).at[d * GLA_GATE_RANK:(d + 1) * GLA_GATE_RANK].set(gla_w_gate[l, d])
            gla_dir.append((wg, row(gla_b_gate[l, d])))
        ng = row(gla_norm_g[l])
        zero_state = jnp.zeros((GLA_HEADS, GLA_DV, GLA_DK), F32)
        oc_f, sc_f = _gla_pass(zc, zc_lr, None, None, *gla_dir[0], tri[(tb_c, False)], zero_state, reverse=False)
        gla_c, sc_b = _gla_pass(zc, zc_lr, None, None, *gla_dir[1], tri[(tb_c, True)], zero_state, reverse=True,
                                oprev=oc_f, norm_g=ng)
        ol_f, _ = _gla_pass(zx, zx_lr, cos, sin, *gla_dir[0], tri[(tb_l, False)], sc_f, reverse=False)
        gla_x, _ = _gla_pass(zx, zx_lr, cos, sin, *gla_dir[1], tri[(tb_l, True)], sc_b, reverse=True,
                             oprev=ol_f, norm_g=ng)

        qg, kg = row(na_q_norm_g[l]), row(na_k_norm_g[l])
        ql, kl, vl = _na_prep(zx, qg, kg)
        qc, kc, vc = _na_prep(zc, qg, kg)
        na_x = _na_latent(ql, kl, vl, kc, vc, _na_bias(na_rpb[l]))

        lru_dir = []
        for d in range(2):
            lru_dir.append(dict(conv_w=lru_conv_w[l], conv_b=row(lru_conv_b[l]),
                                w_a=lru_w_a[l, d].astype(BF16), b_a=row(lru_b_a[l, d]),
                                w_i=lru_w_i[l, d].astype(BF16), b_i=row(lru_b_i[l, d]),
                                lam=row(lru_lambda[l, d])))
        zero_h = jnp.zeros((1, LRU_WIDTH), F32)
        hc_f, fin_f = _lru_pass(zc, lru_dir[0], zero_h, reverse=False)
        lru_c, fin_b = _lru_pass(zc, lru_dir[1], zero_h, reverse=True, hprev=hc_f)
        hl_f, _ = _lru_pass(zx, lru_dir[0], fin_f[0:1], reverse=False)
        lru_x, _ = _lru_pass(zx, lru_dir[1], fin_b[0:1], reverse=True, hprev=hl_f)

        wo = w_out[l].astype(BF16)
        wr = jnp.zeros((D, LANES), BF16).at[:, :N_EXPERTS].set(w_router[l].astype(BF16))
        g2 = row(norm2_g[l])
        x1, affx = _outproj(gla_x, na_x, lru_x, xs, wo, mx[2], g2, mx[3], mx[4], wr)
        xg, twx, idxx = _route(affx, x1, g2, mx[3], mx[4])
        if ctx_out:
            na_c = _na_context(qc, kc, vc)
            c1, affc = _outproj(gla_c, na_c, lru_c, cs, wo, mc[2], g2, mc[3], mc[4], wr)
            cg, twc, idxc = _route(affc, c1, g2, mc[3], mc[4])
            ye, yc = _moe(xg, twx, mx[5], l, w_exp_gate, w_exp_up, w_exp_down, ctx=(cg, twc, mc[5]))
            cs = _combine(c1, yc, idxc)
        else:
            ye, _ = _moe(xg, twx, mx[5], l, w_exp_gate, w_exp_up, w_exp_down)
        xs = _combine(x1, ye, idxx)
    return xs[None]
```
